```python
import jax, jax.numpy as jnp
from jax import lax
import numpy as np

D_MODEL = 1024
BATCH = 4
SEQ = 8192
DEPTH = 4

MEM_LEN = 256
RET_HEADS = 4
RET_QK_DIM = 128
RET_V_DIM = 128
RET_WIDTH = RET_HEADS * RET_QK_DIM
RET_CHUNK = 128
ROPE_BASE = 10000.0
POOL_WINDOWS = (2, 4, 8, 16)
POOL_GROUPS = 4
POOL_GROUP_DIM = D_MODEL // 16
POOL_WIDTH = POOL_GROUPS * POOL_GROUP_DIM
XA_HEADS = 4
XA_HEAD_DIM = D_MODEL // 16
XA_WIDTH = XA_HEADS * XA_HEAD_DIM
N_BRANCH = 3
IN_SPLITS = (RET_WIDTH, RET_WIDTH, RET_HEADS * RET_V_DIM, RET_HEADS * RET_V_DIM, POOL_WIDTH, XA_WIDTH, N_BRANCH * D_MODEL)
IN_WIDTH = sum(IN_SPLITS)
FFN_HIDDEN = -(-8 * D_MODEL // (3 * 256)) * 256
EPS = 1e-6

kernel_name = "hybrid_retention_pool_memxattn_block"


def rmsnorm(x, g):
    xf = x.astype(jnp.float32)
    y = xf * lax.rsqrt(jnp.mean(xf * xf, axis=-1, keepdims=True) + EPS)
    return (y * g.astype(jnp.float32)).astype(x.dtype)


def rope(t, cos, sin):
    t1, t2 = jnp.split(t, 2, axis=-1)
    return jnp.concatenate([t1 * cos - t2 * sin, t2 * cos + t1 * sin], axis=-1)


def retention_chunkwise(q, k, v):
    B, S, H, Dk = q.shape
    Dv = v.shape[-1]
    C = RET_CHUNK
    N = S // C
    q = q.reshape(B, N, C, H, Dk)
    k = k.reshape(B, N, C, H, Dk)
    v = v.reshape(B, N, C, H, Dv)
    log_g = jnp.log(1.0 - jnp.exp2(-5.0 - jnp.arange(H, dtype=jnp.float32)))
    idx = jnp.arange(C, dtype=jnp.float32)
    diff = idx[:, None] - idx[None, :]
    decay_in = jnp.where(diff[None] >= 0, jnp.exp(jnp.maximum(diff, 0.0)[None] * log_g[:, None, None]), 0.0)
    scores = jnp.einsum('bnihd,bnjhd->bnhij', q, k) * decay_in[None, None]
    inner = jnp.einsum('bnhij,bnjhe->bnihe', scores, v)
    zeta = jnp.exp((C - 1.0 - idx)[None, :] * log_g[:, None])
    chunk_kv = jnp.einsum('bnjhd,bnjhe,hj->bnhde', k, v, zeta)
    g_chunk = jnp.exp(C * log_g)[None, :, None, None]

    def step(R, kv):
        return R * g_chunk + kv, R

    _, R_prev = lax.scan(step, jnp.zeros((B, H, Dk, Dv), jnp.float32), jnp.moveaxis(chunk_kv, 1, 0))
    R_prev = jnp.moveaxis(R_prev, 0, 1)
    xi = jnp.exp((idx + 1.0)[None, :] * log_g[:, None])
    cross = jnp.einsum('bnihd,bnhde,hi->bnihe', q, R_prev, xi)
    return (inner + cross).reshape(B, S, H, Dv)


def multiscale_pool(u):
    B, S, _ = u.shape
    uf = u.astype(jnp.float32).reshape(B, S, POOL_GROUPS, POOL_GROUP_DIM)
    cpad = jnp.pad(jnp.cumsum(uf, axis=1), ((0, 0), (1, 0), (0, 0), (0, 0)))
    t = jnp.arange(S)
    outs = []
    for g, w in enumerate(POOL_WINDOWS):
        c = cpad[:, :, g]
        lagged = jnp.pad(c[:, :S + 1 - w], ((0, 0), (w - 1, 0), (0, 0)))
        count = jnp.minimum(t + 1, w).astype(jnp.float32)[None, :, None]
        outs.append((c[:, 1:] - lagged) / count - uf[:, :, g])
    return jnp.stack(outs, axis=2)


def memory_xattn(q, mem, g_mem, w_mem_kv):
    B, S, _ = q.shape
    M = mem.shape[1]
    kv = rmsnorm(mem, g_mem) @ w_mem_kv
    mk, mv = jnp.split(kv, 2, axis=-1)
    qh = q.reshape(B, S, XA_HEADS, XA_HEAD_DIM)
    mk = mk.reshape(B, M, XA_HEADS, XA_HEAD_DIM)
    mv = mv.reshape(B, M, XA_HEADS, XA_HEAD_DIM)
    s = jnp.einsum('bshd,bmhd->bhsm', qh, mk).astype(jnp.float32) * (XA_HEAD_DIM ** -0.5)
    p = jax.nn.softmax(s, axis=-1).astype(mv.dtype)
    return jnp.einsum('bhsm,bmhd->bshd', p, mv).reshape(B, S, XA_WIDTH)


def setup_inputs(seed: int = 0) -> dict:
    key = jax.random.key(seed)
    ks = jax.random.split(key, 20)
    nrm = lambda k, shape, fan: jax.random.normal(k, shape, jnp.float32) * (fan ** -0.5)
    gain = lambda k, shape: 1.0 + 0.02 * jax.random.normal(k, shape, jnp.float32)
    return {
        "x": jax.random.normal(ks[0], (BATCH, SEQ, D_MODEL), jnp.float32),
        "mem": jax.random.normal(ks[1], (BATCH, MEM_LEN, D_MODEL), jnp.float32),
        "positions": jnp.arange(SEQ, dtype=jnp.int32),
        "norm_mix": gain(ks[2], (DEPTH, D_MODEL)),
        "w_in": nrm(ks[3], (DEPTH, D_MODEL, IN_WIDTH), D_MODEL),
        "w_up_ret": nrm(ks[4], (DEPTH, RET_HEADS * RET_V_DIM, D_MODEL), RET_HEADS * RET_V_DIM),
        "w_pool_mix": nrm(ks[5], (DEPTH, POOL_GROUPS, POOL_GROUP_DIM, POOL_GROUP_DIM), POOL_GROUP_DIM),
        "pool_scale": 1.0 + 0.1 * jax.random.normal(ks[6], (DEPTH, POOL_WIDTH), jnp.float32),
        "w_up_pool": nrm(ks[7], (DEPTH, POOL_WIDTH, D_MODEL), POOL_WIDTH),
        "norm_mem": gain(ks[8], (DEPTH, D_MODEL)),
        "w_mem_kv": nrm(ks[9], (DEPTH, D_MODEL, 2 * XA_WIDTH), D_MODEL),
        "w_up_x": nrm(ks[10], (DEPTH, XA_WIDTH, D_MODEL), XA_WIDTH),
        "w_out": nrm(ks[11], (DEPTH, D_MODEL, D_MODEL), D_MODEL),
        "norm_ffn": gain(ks[12], (DEPTH, D_MODEL)),
        "w_ffn_in": nrm(ks[13], (DEPTH, D_MODEL, 2 * FFN_HIDDEN), D_MODEL),
        "w_ffn_out": nrm(ks[14], (DEPTH, FFN_HIDDEN, D_MODEL), FFN_HIDDEN),
        "final_norm": gain(ks[15], (D_MODEL,)),
    }


def reference(x, mem, positions, norm_mix, w_in, w_up_ret, w_pool_mix, pool_scale, w_up_pool,
              norm_mem, w_mem_kv, w_up_x, w_out, norm_ffn, w_ffn_in, w_ffn_out, final_norm):
    B, S, D = x.shape
    inv_freq = ROPE_BASE ** (-jnp.arange(0, RET_QK_DIM, 2, dtype=jnp.float32) / RET_QK_DIM)
    ang = positions.astype(jnp.float32)[:, None] * inv_freq[None, :]
    cos = jnp.cos(ang)[None, :, None, :]
    sin = jnp.sin(ang)[None, :, None, :]
    split_pts = np.cumsum(IN_SPLITS)[:-1].tolist()

    for l in range(DEPTH):
        h = rmsnorm(x, norm_mix[l])
        z = h @ w_in[l]
        q, k, v, g_ret, u_pool, q_x, gates = jnp.split(z, split_pts, axis=-1)

        qr = rope(q.reshape(B, S, RET_HEADS, RET_QK_DIM).astype(jnp.float32), cos, sin)
        kr = rope(k.reshape(B, S, RET_HEADS, RET_QK_DIM).astype(jnp.float32), cos, sin) * (RET_QK_DIM ** -0.5)
        vr = v.reshape(B, S, RET_HEADS, RET_V_DIM).astype(jnp.float32)
        y = retention_chunkwise(qr, kr, vr)
        y = y * lax.rsqrt(jnp.mean(y * y, axis=-1, keepdims=True) + EPS)
        y_ret = (y.reshape(B, S, RET_HEADS * RET_V_DIM) * jax.nn.silu(g_ret.astype(jnp.float32))).astype(x.dtype)
        b_ret = y_ret @ w_up_ret[l]

        pooled = multiscale_pool(u_pool)
        mixed = jnp.einsum('bsgc,gcd->bsgd', pooled, w_pool_mix[l].astype(jnp.float32))
        y_pool = (mixed.reshape(B, S, POOL_WIDTH) * pool_scale[l].astype(jnp.float32)).astype(x.dtype)
        b_pool = y_pool @ w_up_pool[l]

        b_x = memory_xattn(q_x, mem, norm_mem[l], w_mem_kv[l]) @ w_up_x[l]

        gt = jax.nn.sigmoid(gates.reshape(B, S, N_BRANCH, D))
        merged = gt[:, :, 0] * b_ret + gt[:, :, 1] * b_pool + gt[:, :, 2] * b_x
        x = x + merged @ w_out[l]

        h2 = rmsnorm(x, norm_ffn[l])
        a, bgate = jnp.split(h2 @ w_ffn_in[l], 2, axis=-1)
        x = x + (jax.nn.silu(a) * bgate) @ w_ffn_out[l]

    return rmsnorm(x, final_norm)
```

```python
import functools

import jax
import jax.numpy as jnp
from jax import lax
from jax.experimental import pallas as pl
from jax.experimental.pallas import tpu as pltpu

D_MODEL = 1024
MEM_LEN = 256
RET_HEADS = 4
RET_DIM = 128
RET_WIDTH = RET_HEADS * RET_DIM
RET_CHUNK = 128
ROPE_BASE = 10000.0
POOL_WINDOWS = (2, 4, 8, 16)
POOL_GROUP_DIM = 64
POOL_WIDTH = 256
POOL_HIST = 32
XA_HEADS = 4
XA_HEAD_DIM = 64
XA_WIDTH = 256
FFN_HIDDEN = 2816
EPS = 1e-6

OFF_Q = 0
OFF_K = OFF_Q + RET_WIDTH
OFF_V = OFF_K + RET_WIDTH
OFF_G = OFF_V + RET_WIDTH
OFF_U = OFF_G + RET_WIDTH
OFF_QX = OFF_U + POOL_WIDTH
OFF_GATES = OFF_QX + XA_WIDTH
IN_WIDTH = OFF_GATES + 3 * D_MODEL

SEQ_TILE = 512
FFN_TILE = 512
FFN_COLS = 256
VMEM_LIMIT_BYTES = 56 * 1024 * 1024

BF16 = jnp.bfloat16
F32 = jnp.float32


def _dot(a, b):
    return jnp.dot(a, b, preferred_element_type=F32)


def _rmsnorm_rows(x, g):
    return x * lax.rsqrt(jnp.mean(x * x, axis=-1, keepdims=True) + EPS) * g


def _memkv_body(mem_ref, g_ref, w_ref, kv_ref):
    m = _rmsnorm_rows(mem_ref[0], g_ref[0]).astype(BF16)
    kv = _dot(m, w_ref[0])
    kv_ref[0, 0, :, :XA_WIDTH] = (kv[:, :XA_WIDTH] * (XA_HEAD_DIM ** -0.5)).astype(BF16)
    kv_ref[0, 0, :, XA_WIDTH:] = kv[:, XA_WIDTH:].astype(BF16)


def _memkv(mem, norm_mem, w_mem_kv):
    depth = norm_mem.shape[0]
    batch = mem.shape[0]
    return pl.pallas_call(
        _memkv_body,
        grid=(depth, batch),
        in_specs=[
            pl.BlockSpec((1, MEM_LEN, D_MODEL), lambda l, b: (b, 0, 0)),
            pl.BlockSpec((1, 1, D_MODEL), lambda l, b: (l, 0, 0)),
            pl.BlockSpec((1, D_MODEL, 2 * XA_WIDTH), lambda l, b: (l, 0, 0)),
        ],
        out_specs=pl.BlockSpec((1, 1, MEM_LEN, 2 * XA_WIDTH), lambda l, b: (l, b, 0, 0)),
        out_shape=jax.ShapeDtypeStruct((depth, batch, MEM_LEN, 2 * XA_WIDTH), BF16),
        name="memkv",
    )(mem, norm_mem.reshape(depth, 1, D_MODEL), w_mem_kv)


def _mixer_body(x_ref, cos_ref, sin_ref, nmix_ref, w_in_ref, w_up_ret_ref, w_pool_ref, pscale_ref,
                w_up_pool_ref, mk_ref, mv_ref, w_up_x_ref, w_out_ref, decay_ref, zeta_ref, xi_ref,
                gch_ref, o_ref, state_scr, hist_scr, ext_a, ext_b, q_scr, k_scr, v_scr, y_scr):
    ts = x_ref.shape[1]
    j = pl.program_id(1)

    @pl.when(j == 0)
    def _start_of_sequence():
        state_scr[...] = jnp.zeros_like(state_scr)
        hist_scr[...] = jnp.zeros_like(hist_scr)

    x = x_ref[0]
    h = _rmsnorm_rows(x, nmix_ref[...]).astype(BF16)

    cos2 = cos_ref[...]
    sin2 = sin_ref[...]
    qk = _dot(h, w_in_ref[:, OFF_Q:OFF_V])
    for hd in range(RET_HEADS):
        lo = hd * RET_DIM
        qh = qk[:, lo:lo + RET_DIM]
        q_scr[:, lo:lo + RET_DIM] = (qh * cos2 + pltpu.roll(qh, RET_DIM // 2, 1) * sin2).astype(BF16)
        kh = qk[:, RET_WIDTH + lo:RET_WIDTH + lo + RET_DIM]
        kr = (kh * cos2 + pltpu.roll(kh, RET_DIM // 2, 1) * sin2) * (RET_DIM ** -0.5)
        k_scr[:, lo:lo + RET_DIM] = kr.astype(BF16)
    v_scr[...] = _dot(h, w_in_ref[:, OFF_V:OFF_G]).astype(BF16)

    for hd in range(RET_HEADS):
        lo = hd * RET_DIM
        state = state_scr[hd]
        decay = decay_ref[hd]
        zeta = zeta_ref[hd]
        xi = xi_ref[hd]
        gch = gch_ref[hd]
        for c in range(ts // RET_CHUNK):
            r0 = c * RET_CHUNK
            q = q_scr[r0:r0 + RET_CHUNK, lo:lo + RET_DIM]
            k = k_scr[r0:r0 + RET_CHUNK, lo:lo + RET_DIM]
            v = v_scr[r0:r0 + RET_CHUNK, lo:lo + RET_DIM]
            scores = lax.dot_general(q, k, (((1,), (1,)), ((), ())), preferred_element_type=F32) * decay
            inner = _dot(scores.astype(BF16), v)
            cross = _dot(q, state.astype(BF16)) * xi
            y_scr[r0:r0 + RET_CHUNK, lo:lo + RET_DIM] = inner + cross
            kz = (k.astype(F32) * zeta).astype(BF16)
            chunk_kv = lax.dot_general(kz, v, (((0,), (0,)), ((), ())), preferred_element_type=F32)
            state = state * gch + chunk_kv
        state_scr[hd] = state

    g_ret = _dot(h, w_in_ref[:, OFF_G:OFF_U])
    y_parts = []
    for hd in range(RET_HEADS):
        lo = hd * RET_DIM
        yh = y_scr[:, lo:lo + RET_DIM]
        yh = yh * lax.rsqrt(jnp.mean(yh * yh, axis=-1, keepdims=True) + EPS)
        gh = g_ret[:, lo:lo + RET_DIM]
        y_parts.append((yh * (gh * jax.nn.sigmoid(gh))).astype(BF16))
    y_ret = jnp.concatenate(y_parts, axis=1)
    b_ret = _dot(y_ret, w_up_ret_ref[...])
    gate = jax.nn.sigmoid(_dot(h, w_in_ref[:, OFF_GATES:OFF_GATES + D_MODEL]))
    merged = gate * b_ret

    u = _dot(h, w_in_ref[:, OFF_U:OFF_QX])
    n_ext = POOL_HIST + ts
    ext_a[0:POOL_HIST, :] = hist_scr[...]
    ext_a[POOL_HIST:n_ext, :] = u
    hist_scr[...] = u[ts - POOL_HIST:, :]
    s2 = ext_a[8:n_ext, :] + ext_a[7:n_ext - 1, :]
    ext_b[8:n_ext, :] = s2
    s4 = ext_b[16:n_ext, :] + ext_b[14:n_ext - 2, :]
    ext_a[16:n_ext, :] = s4
    s8 = ext_a[24:n_ext, :] + ext_a[20:n_ext - 4, :]
    ext_b[24:n_ext, :] = s8
    s16 = ext_b[32:n_ext, :] + ext_b[24:n_ext - 8, :]
    lane = lax.broadcasted_iota(jnp.int32, (ts, POOL_WIDTH), 1)
    row = lax.broadcasted_iota(jnp.int32, (ts, POOL_WIDTH), 0)
    sums = jnp.where(lane < 64, s2[POOL_HIST - 8:, :],
                     jnp.where(lane < 128, s4[POOL_HIST - 16:, :],
                               jnp.where(lane < 192, s8[POOL_HIST - 24:, :], s16)))
    window = jnp.left_shift(2, lax.shift_right_logical(lane, 6))
    count = jnp.minimum(j * ts + row + 1, window).astype(F32)
    pooled = sums / count - u
    mixed = _dot(pooled.astype(BF16), w_pool_ref[...]) * pscale_ref[...]
    b_pool = _dot(mixed.astype(BF16), w_up_pool_ref[...])
    gate = jax.nn.sigmoid(_dot(h, w_in_ref[:, OFF_GATES + D_MODEL:OFF_GATES + 2 * D_MODEL]))
    merged = merged + gate * b_pool

    qx = _dot(h, w_in_ref[:, OFF_QX:OFF_GATES]).astype(BF16)
    s_all = _dot(qx, mk_ref[0, 0])
    p_parts = []
    for hd in range(XA_HEADS):
        s = s_all[:, hd * MEM_LEN:(hd + 1) * MEM_LEN]
        e = jnp.exp(s - jnp.max(s, axis=-1, keepdims=True))
        p_parts.append((e * (1.0 / jnp.sum(e, axis=-1, keepdims=True))).astype(BF16))
    p = jnp.concatenate(p_parts, axis=1)
    att = _dot(p, mv_ref[0, 0])
    b_x = _dot(att.astype(BF16), w_up_x_ref[...])
    gate = jax.nn.sigmoid(_dot(h, w_in_ref[:, OFF_GATES + 2 * D_MODEL:IN_WIDTH]))
    merged = merged + gate * b_x

    o_ref[0] = x + _dot(merged.astype(BF16), w_out_ref[...])


def _resident(shape, index_map):
    return pl.BlockSpec(shape, index_map, pipeline_mode=pl.Buffered(1))


def _mixer(x, cos2, sin2, layer, nmix, w_in, w_up_ret, w_pool_bd, pscale, w_up_pool, mk_bd, mv_bd,
           w_up_x, w_out, decay, zeta, xi, gch):
    batch, seq, _ = x.shape
    ts = SEQ_TILE
    l = layer
    whole = lambda dims: (lambda b, j: (l,) + (0,) * dims)
    const = lambda dims: (lambda b, j: (0,) * dims)
    in_specs = [
        pl.BlockSpec((1, ts, D_MODEL), lambda b, j: (b, j, 0)),
        pl.BlockSpec((ts, RET_DIM), lambda b, j: (j, 0)),
        pl.BlockSpec((ts, RET_DIM), lambda b, j: (j, 0)),
        _resident((None, 1, D_MODEL), whole(2)),
        _resident((None, D_MODEL, IN_WIDTH), whole(2)),
        _resident((None, RET_WIDTH, D_MODEL), whole(2)),
        _resident((None, POOL_WIDTH, POOL_WIDTH), whole(2)),
        _resident((None, 1, POOL_WIDTH), whole(2)),
        _resident((None, POOL_WIDTH, D_MODEL), whole(2)),
        pl.BlockSpec((1, 1, XA_WIDTH, XA_HEADS * MEM_LEN), lambda b, j: (l, b, 0, 0)),
        pl.BlockSpec((1, 1, XA_HEADS * MEM_LEN, XA_WIDTH), lambda b, j: (l, b, 0, 0)),
        _resident((None, XA_WIDTH, D_MODEL), whole(2)),
        _resident((None, D_MODEL, D_MODEL), whole(2)),
        _resident((RET_HEADS, RET_CHUNK, RET_CHUNK), const(3)),
        _resident((RET_HEADS, RET_CHUNK, RET_DIM), const(3)),
        _resident((RET_HEADS, RET_CHUNK, RET_DIM), const(3)),
        _resident((RET_HEADS, 1, RET_DIM), const(3)),
    ]
    scratch = [
        pltpu.VMEM((RET_HEADS, RET_DIM, RET_DIM), F32),
        pltpu.VMEM((POOL_HIST, POOL_WIDTH), F32),
        pltpu.VMEM((POOL_HIST + ts, POOL_WIDTH), F32),
        pltpu.VMEM((POOL_HIST + ts, POOL_WIDTH), F32),
        pltpu.VMEM((ts, RET_WIDTH), BF16),
        pltpu.VMEM((ts, RET_WIDTH), BF16),
        pltpu.VMEM((ts, RET_WIDTH), BF16),
        pltpu.VMEM((ts, RET_WIDTH), F32),
    ]
    return pl.pallas_call(
        _mixer_body,
        grid=(batch, seq // ts),
        in_specs=in_specs,
        out_specs=pl.BlockSpec((1, ts, D_MODEL), lambda b, j: (b, j, 0)),
        out_shape=jax.ShapeDtypeStruct(x.shape, x.dtype),
        scratch_shapes=scratch,
        compiler_params=pltpu.CompilerParams(
            dimension_semantics=("arbitrary", "arbitrary"), vmem_limit_bytes=VMEM_LIMIT_BYTES),
        name="mixer",
    )(x, cos2, sin2, nmix, w_in, w_up_ret, w_pool_bd, pscale, w_up_pool, mk_bd, mv_bd, w_up_x, w_out,
      decay, zeta, xi, gch)


def _ffn_body(x_ref, g_ref, wa_ref, wb_ref, wo_ref, fin_ref, o_ref, hid_scr, *, final_norm):
    x = x_ref[...]
    h = _rmsnorm_rows(x, g_ref[...]).astype(BF16)
    for n in range(FFN_HIDDEN // FFN_COLS):
        cols = slice(n * FFN_COLS, (n + 1) * FFN_COLS)
        a = _dot(h, wa_ref[:, cols])
        b = _dot(h, wb_ref[:, cols])
        hid_scr[:, cols] = (a * jax.nn.sigmoid(a) * b).astype(BF16)
    y = x + _dot(hid_scr[...], wo_ref[...])
    if final_norm:
        y = _rmsnorm_rows(y, fin_ref[...])
    o_ref[...] = y


def _ffn(x, layer, norm_ffn, w_ffn_in, w_ffn_out, final_gain, final_norm):
    tokens = x.shape[0]
    ts = FFN_TILE
    l = layer
    return pl.pallas_call(
        functools.partial(_ffn_body, final_norm=final_norm),
        grid=(tokens // ts,),
        in_specs=[
            pl.BlockSpec((ts, D_MODEL), lambda i: (i, 0)),
            _resident((None, 1, D_MODEL), lambda i: (l, 0, 0)),
            _resident((None, D_MODEL, FFN_HIDDEN), lambda i: (l, 0, 0)),
            _resident((None, D_MODEL, FFN_HIDDEN), lambda i: (l, 0, 1)),
            _resident((None, FFN_HIDDEN, D_MODEL), lambda i: (l, 0, 0)),
            _resident((1, D_MODEL), lambda i: (0, 0)),
        ],
        out_specs=pl.BlockSpec((ts, D_MODEL), lambda i: (i, 0)),
        out_shape=jax.ShapeDtypeStruct(x.shape, x.dtype),
        scratch_shapes=[pltpu.VMEM((ts, FFN_HIDDEN), BF16)],
        compiler_params=pltpu.CompilerParams(
            dimension_semantics=("arbitrary",), vmem_limit_bytes=VMEM_LIMIT_BYTES),
        name="ffn_final" if final_norm else "ffn",
    )(x, norm_ffn, w_ffn_in, w_ffn_in, w_ffn_out, final_gain)


def _retention_tables():
    heads = jnp.arange(RET_HEADS, dtype=F32)
    log_g = jnp.log(1.0 - jnp.exp2(-5.0 - heads))
    idx = jnp.arange(RET_CHUNK, dtype=F32)
    diff = idx[:, None] - idx[None, :]
    decay = jnp.where(diff[None] >= 0, jnp.exp(jnp.maximum(diff, 0.0)[None] * log_g[:, None, None]), 0.0)
    zeta = jnp.exp((RET_CHUNK - 1.0 - idx)[None, :] * log_g[:, None])
    xi = jnp.exp((idx + 1.0)[None, :] * log_g[:, None])
    gch = jnp.exp(RET_CHUNK * log_g)
    rows = lambda t: jnp.broadcast_to(t[:, :, None], (RET_HEADS, RET_CHUNK, RET_DIM))
    return decay, rows(zeta), rows(xi), jnp.broadcast_to(gch[:, None, None], (RET_HEADS, 1, RET_DIM))


def kernel(x, mem, positions, norm_mix, w_in, w_up_ret, w_pool_mix, pool_scale, w_up_pool, norm_mem,
           w_mem_kv, w_up_x, w_out, norm_ffn, w_ffn_in, w_ffn_out, final_norm):
    batch, seq, d = x.shape
    depth = w_in.shape[0]
    assert d == D_MODEL and seq % SEQ_TILE == 0 and (batch * seq) % FFN_TILE == 0
    assert mem.shape == (batch, MEM_LEN, D_MODEL) and w_in.shape[2] == IN_WIDTH

    inv_freq = ROPE_BASE ** (-jnp.arange(0, RET_DIM, 2, dtype=F32) / RET_DIM)
    ang = positions.astype(F32)[:, None] * inv_freq[None, :]
    cos2 = jnp.concatenate([jnp.cos(ang), jnp.cos(ang)], axis=-1)
    sin2 = jnp.concatenate([-jnp.sin(ang), jnp.sin(ang)], axis=-1)
    decay, zeta, xi, gch = _retention_tables()

    to_bf16 = lambda w: w.astype(BF16)
    w_in_b, w_up_ret_b, w_up_pool_b, w_up_x_b, w_out_b = map(to_bf16, (w_in, w_up_ret, w_up_pool, w_up_x, w_out))
    w_ffn_in_b, w_ffn_out_b, w_mem_kv_b = map(to_bf16, (w_ffn_in, w_ffn_out, w_mem_kv))
    eye_g = jnp.eye(len(POOL_WINDOWS), dtype=F32)
    w_pool_bd = jnp.einsum('lgcd,gh->lgchd', w_pool_mix, eye_g).reshape(depth, POOL_WIDTH, POOL_WIDTH).astype(BF16)

    kv = _memkv(mem, norm_mem, w_mem_kv_b)
    mk = kv[..., :XA_WIDTH].reshape(depth, batch, MEM_LEN, XA_HEADS, XA_HEAD_DIM)
    mv = kv[..., XA_WIDTH:].reshape(depth, batch, MEM_LEN, XA_HEADS, XA_HEAD_DIM)
    eye_h = jnp.eye(XA_HEADS, dtype=BF16)
    mk_bd = jnp.einsum('lbmhd,hg->lbhdgm', mk, eye_h).reshape(depth, batch, XA_WIDTH, XA_HEADS * MEM_LEN)
    mv_bd = jnp.einsum('lbmhd,hg->lbhmgd', mv, eye_h).reshape(depth, batch, XA_HEADS * MEM_LEN, XA_WIDTH)

    nmix = norm_mix.reshape(depth, 1, D_MODEL)
    nffn = norm_ffn.reshape(depth, 1, D_MODEL)
    pscale = pool_scale.reshape(depth, 1, POOL_WIDTH)
    fin = final_norm.reshape(1, D_MODEL)

    for l in range(depth):
        x = _mixer(x, cos2, sin2, l, nmix, w_in_b, w_up_ret_b, w_pool_bd, pscale, w_up_pool_b, mk_bd, mv_bd,
                   w_up_x_b, w_out_b, decay, zeta, xi, gch)
        x = _ffn(x.reshape(batch * seq, d), l, nffn, w_ffn_in_b, w_ffn_out_b, fin,
                 final_norm=(l == depth - 1)).reshape(batch, seq, d)
    return x
```

```python
import functools

import jax
import jax.numpy as jnp
from jax import lax
from jax.experimental import pallas as pl
from jax.experimental.pallas import tpu as pltpu

D_MODEL = 1024
MEM_LEN = 256
RET_HEADS = 4
RET_DIM = 128
RET_WIDTH = RET_HEADS * RET_DIM
RET_CHUNK = 128
ROPE_BASE = 10000.0
POOL_WINDOWS = (2, 4, 8, 16)
POOL_GROUP_DIM = 64
POOL_WIDTH = 256
POOL_HIST = 32
XA_HEADS = 4
XA_HEAD_DIM = 64
XA_WIDTH = 256
FFN_HIDDEN = 2816
EPS = 1e-6

OFF_Q = 0
OFF_K = OFF_Q + RET_WIDTH
OFF_V = OFF_K + RET_WIDTH
OFF_G = OFF_V + RET_WIDTH
OFF_U = OFF_G + RET_WIDTH
OFF_QX = OFF_U + POOL_WIDTH
OFF_GATES = OFF_QX + XA_WIDTH
IN_WIDTH = OFF_GATES + 3 * D_MODEL

SEQ_TILE = 512
FFN_TILE = 1024
SLAB = 256
VMEM_LIMIT_BYTES = 56 * 1024 * 1024

BF16 = jnp.bfloat16
F32 = jnp.float32


def _dot(a, b):
    return jnp.dot(a, b, preferred_element_type=F32)


def _rmsnorm_rows(x, g):
    return x * lax.rsqrt(jnp.mean(x * x, axis=-1, keepdims=True) + EPS) * g


def _memkv_body(mem_ref, g_ref, w_ref, kv_ref):
    m = _rmsnorm_rows(mem_ref[0], g_ref[0]).astype(BF16)
    kv = _dot(m, w_ref[0])
    kv_ref[0, 0, :, :XA_WIDTH] = (kv[:, :XA_WIDTH] * (XA_HEAD_DIM ** -0.5)).astype(BF16)
    kv_ref[0, 0, :, XA_WIDTH:] = kv[:, XA_WIDTH:].astype(BF16)


def _memkv(mem, norm_mem, w_mem_kv):
    depth = norm_mem.shape[0]
    batch = mem.shape[0]
    return pl.pallas_call(
        _memkv_body,
        grid=(depth, batch),
        in_specs=[
            pl.BlockSpec((1, MEM_LEN, D_MODEL), lambda l, b: (b, 0, 0)),
            pl.BlockSpec((1, 1, D_MODEL), lambda l, b: (l, 0, 0)),
            pl.BlockSpec((1, D_MODEL, 2 * XA_WIDTH), lambda l, b: (l, 0, 0)),
        ],
        out_specs=pl.BlockSpec((1, 1, MEM_LEN, 2 * XA_WIDTH), lambda l, b: (l, b, 0, 0)),
        out_shape=jax.ShapeDtypeStruct((depth, batch, MEM_LEN, 2 * XA_WIDTH), BF16),
        name="memkv",
    )(mem, norm_mem.reshape(depth, 1, D_MODEL), w_mem_kv)


def _mixer_body(x_ref, cos_ref, sin_ref, nmix_ref, w_in_ref, w_up_ret_ref, w_pool_ref, pscale_ref,
                w_up_pool_ref, mk_ref, mv_ref, w_up_x_ref, w_out_ref, decay_ref, zeta_ref, xi_ref,
                gch_ref, o_ref, state_scr, hist_scr, ext_u, ext_a, ext_b, h_scr, q_scr, qxi_scr, k_scr,
                kz_scr, v_scr, y_scr, gsilu_scr, gate_scr, qx_scr, p_scr, sums_scr, merged_scr):
    ts = x_ref.shape[1]
    n_chunks = ts // RET_CHUNK
    n_ext = POOL_HIST + ts
    j = pl.program_id(1)

    @pl.when(j == 0)
    def _start_of_sequence():
        state_scr[...] = jnp.zeros_like(state_scr)
        hist_scr[...] = jnp.zeros_like(hist_scr)

    h_scr[...] = _rmsnorm_rows(x_ref[0], nmix_ref[...]).astype(BF16)

    def project(col0):
        return _dot(h_scr[...], w_in_ref[:, col0:col0 + SLAB])

    cos2 = cos_ref[...]
    sin2 = sin_ref[...]

    def rope(t):
        return t * cos2 + pltpu.roll(t, RET_DIM // 2, 1) * sin2

    def store_rotated(slab, n, plain_scr, scaled_scr, row_scale_ref, mult):
        for i in range(SLAB // RET_DIM):
            hd = n * (SLAB // RET_DIM) + i
            lo = hd * RET_DIM
            t = rope(slab[:, i * RET_DIM:(i + 1) * RET_DIM])
            if mult is not None:
                t = t * mult
            plain_scr[:, lo:lo + RET_DIM] = t.astype(BF16)
            row_scale = row_scale_ref[hd]
            for c in range(n_chunks):
                rows = slice(c * RET_CHUNK, (c + 1) * RET_CHUNK)
                scaled_scr[rows, lo:lo + RET_DIM] = (t[rows] * row_scale).astype(BF16)

    for n in range(RET_WIDTH // SLAB):
        store_rotated(project(OFF_Q + n * SLAB), n, q_scr, qxi_scr, xi_ref, None)
    for n in range(RET_WIDTH // SLAB):
        store_rotated(project(OFF_K + n * SLAB), n, k_scr, kz_scr, zeta_ref, RET_DIM ** -0.5)
    for n in range(RET_WIDTH // SLAB):
        v_scr[:, n * SLAB:(n + 1) * SLAB] = project(OFF_V + n * SLAB).astype(BF16)

    def g_ret_slab(n):
        g = project(OFF_G + n * SLAB)
        gsilu_scr[:, n * SLAB:(n + 1) * SLAB] = g * jax.nn.sigmoid(g)

    def gate_slab(n):
        gate_scr[:, n * SLAB:(n + 1) * SLAB] = jax.nn.sigmoid(project(OFF_GATES + n * SLAB))

    def pool_in_slab():
        u = project(OFF_U)
        ext_u[0:POOL_HIST, :] = hist_scr[...]
        ext_u[POOL_HIST:n_ext, :] = u
        hist_scr[...] = u[ts - POOL_HIST:, :]

    def xattn_q_slab():
        qx_scr[...] = project(OFF_QX).astype(BF16)

    def retention_unit(c, hd):
        rows = slice(c * RET_CHUNK, (c + 1) * RET_CHUNK)
        cols = slice(hd * RET_DIM, (hd + 1) * RET_DIM)
        q = q_scr[rows, cols]
        k = k_scr[rows, cols]
        v = v_scr[rows, cols]
        state = state_scr[hd]
        scores = lax.dot_general(q, k, (((1,), (1,)), ((), ())), preferred_element_type=F32) * decay_ref[hd]
        lhs = jnp.concatenate([scores.astype(BF16), qxi_scr[rows, cols]], axis=1)
        rhs = jnp.concatenate([v, state.astype(BF16)], axis=0)
        y_scr[rows, cols] = _dot(lhs, rhs)
        chunk_kv = lax.dot_general(kz_scr[rows, cols], v, (((0,), (0,)), ((), ())), preferred_element_type=F32)
        state_scr[hd] = state * gch_ref[hd] + chunk_kv

    def xattn_head(hd):
        cols = slice(hd * MEM_LEN, (hd + 1) * MEM_LEN)
        s = _dot(qx_scr[...], mk_ref[0, 0, :, cols])
        e = jnp.exp(s - jnp.max(s, axis=-1, keepdims=True))
        p_scr[:, cols] = (e * (1.0 / jnp.sum(e, axis=-1, keepdims=True))).astype(BF16)

    def xattn_out():
        att = _dot(p_scr[...], mv_ref[0, 0])
        merged_scr[...] = gate_scr[:, 2 * D_MODEL:3 * D_MODEL] * _dot(att.astype(BF16), w_up_x_ref[...])

    def pool_level(k):
        src, dst = (ext_u, ext_a, ext_b, ext_a)[k], (ext_a, ext_b, ext_a, None)[k]
        lo, shift = 8 * (k + 1), 1 << k
        s = src[lo:n_ext, :] + src[lo - shift:n_ext - shift, :]
        if dst is not None:
            dst[lo:n_ext, :] = s
        lane = lax.broadcasted_iota(jnp.int32, (ts, POOL_WIDTH), 1)
        group = (lane >= k * POOL_GROUP_DIM) & (lane < (k + 1) * POOL_GROUP_DIM)
        tile_sum = s[POOL_HIST - lo:, :]
        sums_scr[...] = tile_sum if k == 0 else jnp.where(group, tile_sum, sums_scr[...])

    def pool_out():
        lane = lax.broadcasted_iota(jnp.int32, (ts, POOL_WIDTH), 1)
        row = lax.broadcasted_iota(jnp.int32, (ts, POOL_WIDTH), 0)
        window = jnp.left_shift(2, lax.shift_right_logical(lane, 6))
        count = jnp.minimum(j * ts + row + 1, window).astype(F32)
        pooled = sums_scr[...] / count - ext_u[POOL_HIST:n_ext, :]
        mixed = _dot(pooled.astype(BF16), w_pool_ref[...]) * pscale_ref[...]
        merged_scr[...] += gate_scr[:, D_MODEL:2 * D_MODEL] * _dot(mixed.astype(BF16), w_up_pool_ref[...])

    gates_per_branch = D_MODEL // SLAB
    def both(*fs):
        return lambda: [f() for f in fs]
    tasks = [functools.partial(g_ret_slab, n) for n in range(RET_WIDTH // SLAB)] + [pool_in_slab, xattn_q_slab]
    tasks += [both(functools.partial(xattn_head, hd), functools.partial(gate_slab, 2 * gates_per_branch + hd))
              for hd in range(XA_HEADS)]
    tasks += [xattn_out]
    tasks += [both(functools.partial(pool_level, k), functools.partial(gate_slab, gates_per_branch + k))
              for k in range(len(POOL_WINDOWS))]
    tasks += [pool_out]
    tasks += [functools.partial(gate_slab, n) for n in range(gates_per_branch)]

    units = [(c, hd) for c in range(n_chunks) for hd in range(RET_HEADS)]
    for i, task in enumerate(tasks):
        task()
        if i < len(units):
            retention_unit(*units[i])
    for unit in units[len(tasks):]:
        retention_unit(*unit)

    y_parts = []
    for hd in range(RET_HEADS):
        cols = slice(hd * RET_DIM, (hd + 1) * RET_DIM)
        yh = y_scr[:, cols]
        yh = yh * lax.rsqrt(jnp.mean(yh * yh, axis=-1, keepdims=True) + EPS)
        y_parts.append((yh * gsilu_scr[:, cols]).astype(BF16))
    y_ret = jnp.concatenate(y_parts, axis=1)
    merged = merged_scr[...] + gate_scr[:, 0:D_MODEL] * _dot(y_ret, w_up_ret_ref[...])
    merged = merged.astype(BF16)
    for n in range(D_MODEL // SLAB):
        cols = slice(n * SLAB, (n + 1) * SLAB)
        o_ref[0, :, cols] = x_ref[0, :, cols] + _dot(merged, w_out_ref[:, cols])


def _resident(shape, index_map):
    return pl.BlockSpec(shape, index_map, pipeline_mode=pl.Buffered(1))


def _mixer(x, cos2, sin2, layer, nmix, w_in, w_up_ret, w_pool_bd, pscale, w_up_pool, mk_bd, mv_bd,
           w_up_x, w_out, decay, zeta, xi, gch):
    batch, seq, _ = x.shape
    ts = SEQ_TILE
    l = layer
    whole = lambda dims: (lambda b, j: (l,) + (0,) * dims)
    const = lambda dims: (lambda b, j: (0,) * dims)
    in_specs = [
        pl.BlockSpec((1, ts, D_MODEL), lambda b, j: (b, j, 0)),
        pl.BlockSpec((ts, RET_DIM), lambda b, j: (j, 0)),
        pl.BlockSpec((ts, RET_DIM), lambda b, j: (j, 0)),
        _resident((None, 1, D_MODEL), whole(2)),
        _resident((None, D_MODEL, IN_WIDTH), whole(2)),
        _resident((None, RET_WIDTH, D_MODEL), whole(2)),
        _resident((None, POOL_WIDTH, POOL_WIDTH), whole(2)),
        _resident((None, 1, POOL_WIDTH), whole(2)),
        _resident((None, POOL_WIDTH, D_MODEL), whole(2)),
        pl.BlockSpec((1, 1, XA_WIDTH, XA_HEADS * MEM_LEN), lambda b, j: (l, b, 0, 0)),
        pl.BlockSpec((1, 1, XA_HEADS * MEM_LEN, XA_WIDTH), lambda b, j: (l, b, 0, 0)),
        _resident((None, XA_WIDTH, D_MODEL), whole(2)),
        _resident((None, D_MODEL, D_MODEL), whole(2)),
        _resident((RET_HEADS, RET_CHUNK, RET_CHUNK), const(3)),
        _resident((RET_HEADS, RET_CHUNK, RET_DIM), const(3)),
        _resident((RET_HEADS, RET_CHUNK, RET_DIM), const(3)),
        _resident((RET_HEADS, 1, RET_DIM), const(3)),
    ]
    scratch = [
        pltpu.VMEM((RET_HEADS, RET_DIM, RET_DIM), F32),
        pltpu.VMEM((POOL_HIST, POOL_WIDTH), F32),
        pltpu.VMEM((POOL_HIST + ts, POOL_WIDTH), F32),
        pltpu.VMEM((POOL_HIST + ts, POOL_WIDTH), F32),
        pltpu.VMEM((POOL_HIST + ts, POOL_WIDTH), F32),
        pltpu.VMEM((ts, D_MODEL), BF16),
        pltpu.VMEM((ts, RET_WIDTH), BF16),
        pltpu.VMEM((ts, RET_WIDTH), BF16),
        pltpu.VMEM((ts, RET_WIDTH), BF16),
        pltpu.VMEM((ts, RET_WIDTH), BF16),
        pltpu.VMEM((ts, RET_WIDTH), BF16),
        pltpu.VMEM((ts, RET_WIDTH), F32),
        pltpu.VMEM((ts, RET_WIDTH), F32),
        pltpu.VMEM((ts, 3 * D_MODEL), F32),
        pltpu.VMEM((ts, XA_WIDTH), BF16),
        pltpu.VMEM((ts, XA_HEADS * MEM_LEN), BF16),
        pltpu.VMEM((ts, POOL_WIDTH), F32),
        pltpu.VMEM((ts, D_MODEL), F32),
    ]
    return pl.pallas_call(
        _mixer_body,
        grid=(batch, seq // ts),
        in_specs=in_specs,
        out_specs=pl.BlockSpec((1, ts, D_MODEL), lambda b, j: (b, j, 0)),
        out_shape=jax.ShapeDtypeStruct(x.shape, x.dtype),
        scratch_shapes=scratch,
        compiler_params=pltpu.CompilerParams(
            dimension_semantics=("arbitrary", "arbitrary"), vmem_limit_bytes=VMEM_LIMIT_BYTES),
        name="mixer",
    )(x, cos2, sin2, nmix, w_in, w_up_ret, w_pool_bd, pscale, w_up_pool, mk_bd, mv_bd, w_up_x, w_out,
      decay, zeta, xi, gch)


def _ffn_body(x_ref, g_ref, wa_ref, wb_ref, wo_ref, fin_ref, o_ref, hid_scr, *, final_norm):
    x = x_ref[...]
    h = _rmsnorm_rows(x, g_ref[...]).astype(BF16)
    for n in range(FFN_HIDDEN // SLAB):
        cols = slice(n * SLAB, (n + 1) * SLAB)
        a = _dot(h, wa_ref[:, cols])
        b = _dot(h, wb_ref[:, cols])
        hid_scr[:, cols] = (a * jax.nn.sigmoid(a) * b).astype(BF16)
    y = x + _dot(hid_scr[...], wo_ref[...])
    if final_norm:
        y = _rmsnorm_rows(y, fin_ref[...])
    o_ref[...] = y


def _ffn(x, layer, norm_ffn, w_ffn_in, w_ffn_out, final_gain, final_norm):
    tokens = x.shape[0]
    ts = FFN_TILE
    l = layer
    return pl.pallas_call(
        functools.partial(_ffn_body, final_norm=final_norm),
        grid=(tokens // ts,),
        in_specs=[
            pl.BlockSpec((ts, D_MODEL), lambda i: (i, 0)),
            _resident((None, 1, D_MODEL), lambda i: (l, 0, 0)),
            _resident((None, D_MODEL, FFN_HIDDEN), lambda i: (l, 0, 0)),
            _resident((None, D_MODEL, FFN_HIDDEN), lambda i: (l, 0, 1)),
            _resident((None, FFN_HIDDEN, D_MODEL), lambda i: (l, 0, 0)),
            _resident((1, D_MODEL), lambda i: (0, 0)),
        ],
        out_specs=pl.BlockSpec((ts, D_MODEL), lambda i: (i, 0)),
        out_shape=jax.ShapeDtypeStruct(x.shape, x.dtype),
        scratch_shapes=[pltpu.VMEM((ts, FFN_HIDDEN), BF16)],
        compiler_params=pltpu.CompilerParams(
            dimension_semantics=("arbitrary",), vmem_limit_bytes=VMEM_LIMIT_BYTES),
        name="ffn_final" if final_norm else "ffn",
    )(x, norm_ffn, w_ffn_in, w_ffn_in, w_ffn_out, final_gain)


def _retention_tables():
    heads = jnp.arange(RET_HEADS, dtype=F32)
    log_g = jnp.log(1.0 - jnp.exp2(-5.0 - heads))
    idx = jnp.arange(RET_CHUNK, dtype=F32)
    diff = idx[:, None] - idx[None, :]
    decay = jnp.where(diff[None] >= 0, jnp.exp(jnp.maximum(diff, 0.0)[None] * log_g[:, None, None]), 0.0)
    zeta = jnp.exp((RET_CHUNK - 1.0 - idx)[None, :] * log_g[:, None])
    xi = jnp.exp((idx + 1.0)[None, :] * log_g[:, None])
    gch = jnp.exp(RET_CHUNK * log_g)
    rows = lambda t: jnp.broadcast_to(t[:, :, None], (RET_HEADS, RET_CHUNK, RET_DIM))
    return decay, rows(zeta), rows(xi), jnp.broadcast_to(gch[:, None, None], (RET_HEADS, 1, RET_DIM))


def kernel(x, mem, positions, norm_mix, w_in, w_up_ret, w_pool_mix, pool_scale, w_up_pool, norm_mem,
           w_mem_kv, w_up_x, w_out, norm_ffn, w_ffn_in, w_ffn_out, final_norm):
    batch, seq, d = x.shape
    depth = w_in.shape[0]
    assert d == D_MODEL and seq % SEQ_TILE == 0 and (batch * seq) % FFN_TILE == 0
    assert mem.shape == (batch, MEM_LEN, D_MODEL) and w_in.shape[2] == IN_WIDTH

    inv_freq = ROPE_BASE ** (-jnp.arange(0, RET_DIM, 2, dtype=F32) / RET_DIM)
    ang = positions.astype(F32)[:, None] * inv_freq[None, :]
    cos2 = jnp.concatenate([jnp.cos(ang), jnp.cos(ang)], axis=-1)
    sin2 = jnp.concatenate([-jnp.sin(ang), jnp.sin(ang)], axis=-1)
    decay, zeta, xi, gch = _retention_tables()

    to_bf16 = lambda w: w.astype(BF16)
    w_in_b, w_up_ret_b, w_up_pool_b, w_up_x_b, w_out_b = map(to_bf16, (w_in, w_up_ret, w_up_pool, w_up_x, w_out))
    w_ffn_in_b, w_ffn_out_b, w_mem_kv_b = map(to_bf16, (w_ffn_in, w_ffn_out, w_mem_kv))
    eye_g = jnp.eye(len(POOL_WINDOWS), dtype=F32)
    w_pool_bd = jnp.einsum('lgcd,gh->lgchd', w_pool_mix, eye_g).reshape(depth, POOL_WIDTH, POOL_WIDTH).astype(BF16)

    kv = _memkv(mem, norm_mem, w_mem_kv_b)
    mk = kv[..., :XA_WIDTH].reshape(depth, batch, MEM_LEN, XA_HEADS, XA_HEAD_DIM)
    mv = kv[..., XA_WIDTH:].reshape(depth, batch, MEM_LEN, XA_HEADS, XA_HEAD_DIM)
    eye_h = jnp.eye(XA_HEADS, dtype=BF16)
    mk_bd = jnp.einsum('lbmhd,hg->lbhdgm', mk, eye_h).reshape(depth, batch, XA_WIDTH, XA_HEADS * MEM_LEN)
    mv_bd = jnp.einsum('lbmhd,hg->lbhmgd', mv, eye_h).reshape(depth, batch, XA_HEADS * MEM_LEN, XA_WIDTH)

    nmix = norm_mix.reshape(depth, 1, D_MODEL)
    nffn = norm_ffn.reshape(depth, 1, D_MODEL)
    pscale = pool_scale.reshape(depth, 1, POOL_WIDTH)
    fin = final_norm.reshape(1, D_MODEL)

    for l in range(depth):
        x = _mixer(x, cos2, sin2, l, nmix, w_in_b, w_up_ret_b, w_pool_bd, pscale, w_up_pool_b, mk_bd, mv_bd,
                   w_up_x_b, w_out_b, decay, zeta, xi, gch)
        x = _ffn(x.reshape(batch * seq, d), l, nffn, w_ffn_in_b, w_ffn_out_b, fin,
                 final_norm=(l == depth - 1)).reshape(batch, seq, d)
    return x
```

```python
import functools

import jax
import jax.numpy as jnp
from jax import lax
from jax.experimental import pallas as pl
from jax.experimental.pallas import tpu as pltpu

D_MODEL = 1024
MEM_LEN = 256
RET_HEADS = 4
RET_DIM = 128
RET_WIDTH = RET_HEADS * RET_DIM
RET_CHUNK = 128
ROPE_BASE = 10000.0
POOL_WINDOWS = (2, 4, 8, 16)
POOL_GROUP_DIM = 64
POOL_WIDTH = 256
POOL_HIST = 32
XA_HEADS = 4
XA_HEAD_DIM = 64
XA_WIDTH = 256
FFN_HIDDEN = 2816
EPS = 1e-6

OFF_Q = 0
OFF_K = OFF_Q + RET_WIDTH
OFF_V = OFF_K + RET_WIDTH
OFF_G = OFF_V + RET_WIDTH
OFF_U = OFF_G + RET_WIDTH
OFF_QX = OFF_U + POOL_WIDTH
OFF_GATES = OFF_QX + XA_WIDTH
IN_WIDTH = OFF_GATES + 3 * D_MODEL

SEQ_TILE = 512
FFN_TILE = 1024
SLAB = 256
NORM_CHUNKS = 8
VMEM_LIMIT_BYTES = 56 * 1024 * 1024

BF16 = jnp.bfloat16
F32 = jnp.float32


def _dot(a, b):
    return jnp.dot(a, b, preferred_element_type=F32)


def _rmsnorm_rows(x, g):
    return x * lax.rsqrt(jnp.mean(x * x, axis=-1, keepdims=True) + EPS) * g


def _zero_after_stores(stored):
    words = pltpu.bitcast(stored, jnp.uint32)
    acc = words[0:8]
    for r in range(8, words.shape[0], 8):
        acc = acc | words[r:r + 8]
    tile = acc[:, 0:128]
    for c in range(128, acc.shape[1], 128):
        tile = tile | acc[:, c:c + 128]
    return lax.shift_right_logical(lax.shift_right_logical(tile, jnp.uint32(16)), jnp.uint32(16))


def _memkv_body(mem_ref, g_ref, w_ref, mk_ref, mv_ref):
    m = _rmsnorm_rows(mem_ref[0], g_ref[0]).astype(BF16)
    kv = _dot(m, w_ref[0])
    keys_t = (kv[:, :XA_WIDTH] * (XA_HEAD_DIM ** -0.5)).T
    values = kv[:, XA_WIDTH:]
    head_shift = XA_HEAD_DIM.bit_length() - 1
    feature_head_k = lax.shift_right_logical(lax.broadcasted_iota(jnp.int32, (XA_WIDTH, MEM_LEN), 0), head_shift)
    feature_head_v = lax.shift_right_logical(lax.broadcasted_iota(jnp.int32, (MEM_LEN, XA_WIDTH), 1), head_shift)
    for hd in range(XA_HEADS):
        span = slice(hd * MEM_LEN, (hd + 1) * MEM_LEN)
        mk_ref[0, 0, :, span] = jnp.where(feature_head_k == hd, keys_t, 0.0).astype(BF16)
        mv_ref[0, 0, span, :] = jnp.where(feature_head_v == hd, values, 0.0).astype(BF16)


def _memkv(mem, norm_mem, w_mem_kv):
    depth = norm_mem.shape[0]
    batch = mem.shape[0]
    wide = XA_HEADS * MEM_LEN
    return pl.pallas_call(
        _memkv_body,
        grid=(depth, batch),
        in_specs=[
            pl.BlockSpec((1, MEM_LEN, D_MODEL), lambda l, b: (b, 0, 0)),
            pl.BlockSpec((1, 1, D_MODEL), lambda l, b: (l, 0, 0)),
            pl.BlockSpec((1, D_MODEL, 2 * XA_WIDTH), lambda l, b: (l, 0, 0)),
        ],
        out_specs=[pl.BlockSpec((1, 1, XA_WIDTH, wide), lambda l, b: (l, b, 0, 0)),
                   pl.BlockSpec((1, 1, wide, XA_WIDTH), lambda l, b: (l, b, 0, 0))],
        out_shape=[jax.ShapeDtypeStruct((depth, batch, XA_WIDTH, wide), BF16),
                   jax.ShapeDtypeStruct((depth, batch, wide, XA_WIDTH), BF16)],
        name="memkv",
    )(mem, norm_mem.reshape(depth, 1, D_MODEL), w_mem_kv)


def _mixer_tile(slot, x_ref, xn_ref, cos_ref, sin_ref, nmix_ref, w_in_ref, w_up_ret_ref, w_pool_ref, pscale_ref,
                w_up_pool_ref, mk_ref, mv_ref, w_up_x_ref, w_out_ref, decay_ref, zeta_ref, xi_ref,
                gch_ref, o_ref, state_scr, hist_scr, ext_u, ext_a, ext_b, h0_scr, h1_scr, q_scr, qxi_scr, k_scr,
                kz_scr, v_scr, s_scr, y_scr, gsilu_scr, gate_scr, qx_scr, p_scr, att_scr, sums_scr, mixed_scr,
                merged_scr, mbf_scr):
    ts = x_ref.shape[1]
    n_chunks = ts // RET_CHUNK
    n_ext = POOL_HIST + ts
    j = pl.program_id(1)
    step = pl.program_id(0) * pl.num_programs(1) + j
    h_cur, h_nxt = (h0_scr, h1_scr) if slot == 0 else (h1_scr, h0_scr)

    @pl.when(j == 0)
    def _start_of_sequence():
        state_scr[...] = jnp.zeros_like(state_scr)
        hist_scr[...] = jnp.zeros_like(hist_scr)

    @pl.when(step == 0)
    def _first_input_norm():
        h_cur[...] = _rmsnorm_rows(x_ref[0], nmix_ref[...]).astype(BF16)

    norm_rows = ts // NORM_CHUNKS

    def next_input_norm(c):
        rows = slice(c * norm_rows, (c + 1) * norm_rows)
        h_nxt[rows, :] = _rmsnorm_rows(xn_ref[0, rows, :], nmix_ref[...]).astype(BF16)
        zero = lax.bitcast_convert_type(_zero_after_stores(h_nxt[rows, :]), F32)
        gsilu_scr[8 * c:8 * c + 8, 0:128] = gsilu_scr[8 * c:8 * c + 8, 0:128] + zero

    def project(col0):
        return _dot(h_cur[...], w_in_ref[:, col0:col0 + SLAB])

    cos2 = cos_ref[...]
    sin2 = sin_ref[...]

    def rope(t):
        return t * cos2 + pltpu.roll(t, RET_DIM // 2, 1) * sin2

    def store_rotated(slab, n, plain_scr, scaled_scr, row_scale_ref, mult):
        for i in range(SLAB // RET_DIM):
            hd = n * (SLAB // RET_DIM) + i
            lo = hd * RET_DIM
            t = rope(slab[:, i * RET_DIM:(i + 1) * RET_DIM])
            if mult is not None:
                t = t * mult
            plain_scr[:, lo:lo + RET_DIM] = t.astype(BF16)
            row_scale = row_scale_ref[hd]
            for c in range(n_chunks):
                rows = slice(c * RET_CHUNK, (c + 1) * RET_CHUNK)
                scaled_scr[rows, lo:lo + RET_DIM] = (t[rows] * row_scale).astype(BF16)

    for n in range(RET_WIDTH // SLAB):
        store_rotated(project(OFF_Q + n * SLAB), n, q_scr, qxi_scr, xi_ref, None)
    for n in range(RET_WIDTH // SLAB):
        store_rotated(project(OFF_K + n * SLAB), n, k_scr, kz_scr, zeta_ref, RET_DIM ** -0.5)
    for n in range(RET_WIDTH // SLAB):
        v_scr[:, n * SLAB:(n + 1) * SLAB] = project(OFF_V + n * SLAB).astype(BF16)

    def g_ret_slab(n):
        g = project(OFF_G + n * SLAB)
        gsilu_scr[:, n * SLAB:(n + 1) * SLAB] = g * jax.nn.sigmoid(g)

    def gate_slab(n):
        gate_scr[:, n * SLAB:(n + 1) * SLAB] = jax.nn.sigmoid(project(OFF_GATES + n * SLAB))

    def pool_in_slab():
        u = project(OFF_U)
        ext_u[0:POOL_HIST, :] = hist_scr[...]
        ext_u[POOL_HIST:n_ext, :] = u
        hist_scr[...] = u[ts - POOL_HIST:, :]

    def xattn_q_slab():
        qx_scr[...] = project(OFF_QX).astype(BF16)

    def retention_scores(c, hd):
        rows = slice(c * RET_CHUNK, (c + 1) * RET_CHUNK)
        cols = slice(hd * RET_DIM, (hd + 1) * RET_DIM)
        scores = lax.dot_general(q_scr[rows, cols], k_scr[rows, cols], (((1,), (1,)), ((), ())),
                                 preferred_element_type=F32)
        s_scr[rows, cols] = (scores * decay_ref[hd]).astype(BF16)

    def retention_out(c, hd):
        rows = slice(c * RET_CHUNK, (c + 1) * RET_CHUNK)
        cols = slice(hd * RET_DIM, (hd + 1) * RET_DIM)
        v = v_scr[rows, cols]
        state = state_scr[hd]
        lhs = jnp.concatenate([s_scr[rows, cols], qxi_scr[rows, cols]], axis=1)
        rhs = jnp.concatenate([v, state.astype(BF16)], axis=0)
        y_scr[rows, cols] = _dot(lhs, rhs)
        chunk_kv = lax.dot_general(kz_scr[rows, cols], v, (((0,), (0,)), ((), ())), preferred_element_type=F32)
        state_scr[hd] = state * gch_ref[hd] + chunk_kv

    def xattn_head(hd):
        cols = slice(hd * MEM_LEN, (hd + 1) * MEM_LEN)
        s = _dot(qx_scr[...], mk_ref[0, 0, :, cols])
        e = jnp.exp(s - jnp.max(s, axis=-1, keepdims=True))
        p_scr[:, cols] = (e * (1.0 / jnp.sum(e, axis=-1, keepdims=True))).astype(BF16)

    def xattn_values():
        att_scr[...] = _dot(p_scr[...], mv_ref[0, 0]).astype(BF16)

    def xattn_up():
        merged_scr[...] = gate_scr[:, 2 * D_MODEL:3 * D_MODEL] * _dot(att_scr[...], w_up_x_ref[...])

    def pool_level(k):
        src, dst = (ext_u, ext_a, ext_b, ext_a)[k], (ext_a, ext_b, ext_a, None)[k]
        lo, shift = 8 * (k + 1), 1 << k
        s = src[lo:n_ext, :] + src[lo - shift:n_ext - shift, :]
        if dst is not None:
            dst[lo:n_ext, :] = s
        lane = lax.broadcasted_iota(jnp.int32, (ts, POOL_WIDTH), 1)
        group = (lane >= k * POOL_GROUP_DIM) & (lane < (k + 1) * POOL_GROUP_DIM)
        tile_sum = s[POOL_HIST - lo:, :]
        sums_scr[...] = tile_sum if k == 0 else jnp.where(group, tile_sum, sums_scr[...])

    def pool_mix():
        lane = lax.broadcasted_iota(jnp.int32, (ts, POOL_WIDTH), 1)
        row = lax.broadcasted_iota(jnp.int32, (ts, POOL_WIDTH), 0)
        window = jnp.left_shift(2, lax.shift_right_logical(lane, 6))
        count = jnp.minimum(j * ts + row + 1, window).astype(F32)
        pooled = sums_scr[...] / count - ext_u[POOL_HIST:n_ext, :]
        mixed_scr[...] = (_dot(pooled.astype(BF16), w_pool_ref[...]) * pscale_ref[...]).astype(BF16)

    def pool_up():
        merged_scr[...] += gate_scr[:, D_MODEL:2 * D_MODEL] * _dot(mixed_scr[...], w_up_pool_ref[...])

    gates_per_branch = D_MODEL // SLAB
    part = functools.partial
    tasks = [part(g_ret_slab, n) for n in range(RET_WIDTH // SLAB)] + [pool_in_slab, xattn_q_slab]
    for hd in range(XA_HEADS):
        tasks += [part(xattn_head, hd), part(gate_slab, 2 * gates_per_branch + hd)]
    tasks += [xattn_values]
    first_norm_task = len(tasks)
    for k in range(len(POOL_WINDOWS)):
        tasks += [part(pool_level, k), part(gate_slab, gates_per_branch + k)]
    tasks += [pool_mix, xattn_up, part(gate_slab, 0), pool_up]
    tasks += [part(gate_slab, n) for n in range(1, gates_per_branch)]

    units = [(c, hd) for c in range(n_chunks) for hd in range(RET_HEADS)]
    out_lag = 2
    assert len(tasks) >= len(units) + out_lag
    for i, task in enumerate(tasks):
        task()
        if i < len(units):
            retention_scores(*units[i])
        if out_lag <= i < len(units) + out_lag:
            retention_out(*units[i - out_lag])
        if first_norm_task <= i < first_norm_task + NORM_CHUNKS:
            next_input_norm(i - first_norm_task)

    y_parts = []
    for hd in range(RET_HEADS):
        cols = slice(hd * RET_DIM, (hd + 1) * RET_DIM)
        yh = y_scr[:, cols]
        yh = yh * lax.rsqrt(jnp.mean(yh * yh, axis=-1, keepdims=True) + EPS)
        y_parts.append((yh * gsilu_scr[:, cols]).astype(BF16))
    y_ret = jnp.concatenate(y_parts, axis=1)
    for n in range(D_MODEL // SLAB):
        cols = slice(n * SLAB, (n + 1) * SLAB)
        b_ret = _dot(y_ret, w_up_ret_ref[:, cols])
        mbf_scr[:, cols] = (merged_scr[:, cols] + gate_scr[:, cols] * b_ret).astype(BF16)
    for n in range(D_MODEL // SLAB):
        cols = slice(n * SLAB, (n + 1) * SLAB)
        o_ref[0, :, cols] = x_ref[0, :, cols] + _dot(mbf_scr[...], w_out_ref[:, cols])


def _by_step_parity(tile_fn, step, refs):
    for slot in range(2):
        pl.when(lax.rem(step, 2) == slot)(functools.partial(tile_fn, slot, *refs))


def _mixer_body(*refs):
    _by_step_parity(_mixer_tile, pl.program_id(0) * pl.num_programs(1) + pl.program_id(1), refs)


def _resident(shape, index_map):
    return pl.BlockSpec(shape, index_map, pipeline_mode=pl.Buffered(1))


def _mixer(x, cos2, sin2, layer, nmix, w_in, w_up_ret, w_pool_bd, pscale, w_up_pool, mk_bd, mv_bd,
           w_up_x, w_out, decay, zeta, xi, gch):
    batch, seq, _ = x.shape
    ts = SEQ_TILE
    l = layer
    n_tiles = seq // ts
    whole = lambda dims: (lambda b, j: (l,) + (0,) * dims)

    def next_tile(b, j):
        wraps = j + 1 == n_tiles
        return (jnp.where(wraps, jnp.minimum(b + 1, batch - 1), b), jnp.where(wraps, 0, j + 1), 0)

    const = lambda dims: (lambda b, j: (0,) * dims)
    in_specs = [
        pl.BlockSpec((1, ts, D_MODEL), lambda b, j: (b, j, 0)),
        pl.BlockSpec((1, ts, D_MODEL), next_tile),
        pl.BlockSpec((ts, RET_DIM), lambda b, j: (j, 0)),
        pl.BlockSpec((ts, RET_DIM), lambda b, j: (j, 0)),
        _resident((None, 1, D_MODEL), whole(2)),
        _resident((None, D_MODEL, IN_WIDTH), whole(2)),
        _resident((None, RET_WIDTH, D_MODEL), whole(2)),
        _resident((None, POOL_WIDTH, POOL_WIDTH), whole(2)),
        _resident((None, 1, POOL_WIDTH), whole(2)),
        _resident((None, POOL_WIDTH, D_MODEL), whole(2)),
        pl.BlockSpec((1, 1, XA_WIDTH, XA_HEADS * MEM_LEN), lambda b, j: (l, b, 0, 0)),
        pl.BlockSpec((1, 1, XA_HEADS * MEM_LEN, XA_WIDTH), lambda b, j: (l, b, 0, 0)),
        _resident((None, XA_WIDTH, D_MODEL), whole(2)),
        _resident((None, D_MODEL, D_MODEL), whole(2)),
        _resident((RET_HEADS, RET_CHUNK, RET_CHUNK), const(3)),
        _resident((RET_HEADS, RET_CHUNK, RET_DIM), const(3)),
        _resident((RET_HEADS, RET_CHUNK, RET_DIM), const(3)),
        _resident((RET_HEADS, 1, RET_DIM), const(3)),
    ]
    scratch = [
        pltpu.VMEM((RET_HEADS, RET_DIM, RET_DIM), F32),
        pltpu.VMEM((POOL_HIST, POOL_WIDTH), F32),
        pltpu.VMEM((POOL_HIST + ts, POOL_WIDTH), F32),
        pltpu.VMEM((POOL_HIST + ts, POOL_WIDTH), F32),
        pltpu.VMEM((POOL_HIST + ts, POOL_WIDTH), F32),
        pltpu.VMEM((ts, D_MODEL), BF16),
        pltpu.VMEM((ts, D_MODEL), BF16),
        pltpu.VMEM((ts, RET_WIDTH), BF16),
        pltpu.VMEM((ts, RET_WIDTH), BF16),
        pltpu.VMEM((ts, RET_WIDTH), BF16),
        pltpu.VMEM((ts, RET_WIDTH), BF16),
        pltpu.VMEM((ts, RET_WIDTH), BF16),
        pltpu.VMEM((ts, RET_WIDTH), BF16),
        pltpu.VMEM((ts, RET_WIDTH), F32),
        pltpu.VMEM((ts, RET_WIDTH), F32),
        pltpu.VMEM((ts, 3 * D_MODEL), F32),
        pltpu.VMEM((ts, XA_WIDTH), BF16),
        pltpu.VMEM((ts, XA_HEADS * MEM_LEN), BF16),
        pltpu.VMEM((ts, XA_WIDTH), BF16),
        pltpu.VMEM((ts, POOL_WIDTH), F32),
        pltpu.VMEM((ts, POOL_WIDTH), BF16),
        pltpu.VMEM((ts, D_MODEL), F32),
        pltpu.VMEM((ts, D_MODEL), BF16),
    ]
    return pl.pallas_call(
        _mixer_body,
        grid=(batch, seq // ts),
        in_specs=in_specs,
        out_specs=pl.BlockSpec((1, ts, D_MODEL), lambda b, j: (b, j, 0)),
        out_shape=jax.ShapeDtypeStruct(x.shape, x.dtype),
        scratch_shapes=scratch,
        compiler_params=pltpu.CompilerParams(
            dimension_semantics=("arbitrary", "arbitrary"), vmem_limit_bytes=VMEM_LIMIT_BYTES),
        name="mixer",
    )(x, x, cos2, sin2, nmix, w_in, w_up_ret, w_pool_bd, pscale, w_up_pool, mk_bd, mv_bd, w_up_x, w_out,
      decay, zeta, xi, gch)


def _ffn_tile(slot, x_ref, xn_ref, g_ref, wa_ref, wb_ref, wo_ref, fin_ref, o_ref, h0_scr, h1_scr, hid_scr, *,
              final_norm):
    step = pl.program_id(0)
    h_cur, h_nxt = (h0_scr, h1_scr) if slot == 0 else (h1_scr, h0_scr)

    @pl.when(step == 0)
    def _first_input_norm():
        h_cur[...] = _rmsnorm_rows(x_ref[...], g_ref[...]).astype(BF16)

    n_slabs = FFN_HIDDEN // SLAB
    norm_rows = x_ref.shape[0] // NORM_CHUNKS
    assert n_slabs > NORM_CHUNKS
    for n in range(n_slabs):
        cols = slice(n * SLAB, (n + 1) * SLAB)
        a = _dot(h_cur[...], wa_ref[:, cols])
        b = _dot(h_cur[...], wb_ref[:, cols])
        hid = a * jax.nn.sigmoid(a) * b
        hid_scr[:, cols] = hid.astype(BF16)
        if 1 <= n <= NORM_CHUNKS:
            rows = slice((n - 1) * norm_rows, n * norm_rows)
            h_nxt[rows, :] = _rmsnorm_rows(xn_ref[rows, :], g_ref[...]).astype(BF16)
            zero = lax.bitcast_convert_type(_zero_after_stores(h_nxt[rows, :]), F32)
            zero = jnp.concatenate([zero, zero], axis=0)
            hid_scr[0:16, n * SLAB:n * SLAB + 128] = (hid[0:16, 0:128] + zero).astype(BF16)
    y = x_ref[...] + _dot(hid_scr[...], wo_ref[...])
    if final_norm:
        y = _rmsnorm_rows(y, fin_ref[...])
    o_ref[...] = y


def _ffn_body(*refs, final_norm):
    _by_step_parity(functools.partial(_ffn_tile, final_norm=final_norm), pl.program_id(0), refs)


def _ffn(x, layer, norm_ffn, w_ffn_in, w_ffn_out, final_gain, final_norm):
    tokens = x.shape[0]
    ts = FFN_TILE
    l = layer
    last = tokens // ts - 1
    return pl.pallas_call(
        functools.partial(_ffn_body, final_norm=final_norm),
        grid=(tokens // ts,),
        in_specs=[
            pl.BlockSpec((ts, D_MODEL), lambda i: (i, 0)),
            pl.BlockSpec((ts, D_MODEL), lambda i: (jnp.minimum(i + 1, last), 0)),
            _resident((None, 1, D_MODEL), lambda i: (l, 0, 0)),
            _resident((None, D_MODEL, FFN_HIDDEN), lambda i: (l, 0, 0)),
            _resident((None, D_MODEL, FFN_HIDDEN), lambda i: (l, 0, 1)),
            _resident((None, FFN_HIDDEN, D_MODEL), lambda i: (l, 0, 0)),
            _resident((1, D_MODEL), lambda i: (0, 0)),
        ],
        out_specs=pl.BlockSpec((ts, D_MODEL), lambda i: (i, 0)),
        out_shape=jax.ShapeDtypeStruct(x.shape, x.dtype),
        scratch_shapes=[pltpu.VMEM((ts, D_MODEL), BF16), pltpu.VMEM((ts, D_MODEL), BF16),
                        pltpu.VMEM((ts, FFN_HIDDEN), BF16)],
        compiler_params=pltpu.CompilerParams(
            dimension_semantics=("arbitrary",), vmem_limit_bytes=VMEM_LIMIT_BYTES),
        name="ffn_final" if final_norm else "ffn",
    )(x, x, norm_ffn, w_ffn_in, w_ffn_in, w_ffn_out, final_gain)


def _retention_tables():
    heads = jnp.arange(RET_HEADS, dtype=F32)
    log_g = jnp.log(1.0 - jnp.exp2(-5.0 - heads))
    idx = jnp.arange(RET_CHUNK, dtype=F32)
    diff = idx[:, None] - idx[None, :]
    decay = jnp.where(diff[None] >= 0, jnp.exp(jnp.maximum(diff, 0.0)[None] * log_g[:, None, None]), 0.0)
    zeta = jnp.exp((RET_CHUNK - 1.0 - idx)[None, :] * log_g[:, None])
    xi = jnp.exp((idx + 1.0)[None, :] * log_g[:, None])
    gch = jnp.exp(RET_CHUNK * log_g)
    rows = lambda t: jnp.broadcast_to(t[:, :, None], (RET_HEADS, RET_CHUNK, RET_DIM))
    return decay, rows(zeta), rows(xi), jnp.broadcast_to(gch[:, None, None], (RET_HEADS, 1, RET_DIM))


def kernel(x, mem, positions, norm_mix, w_in, w_up_ret, w_pool_mix, pool_scale, w_up_pool, norm_mem,
           w_mem_kv, w_up_x, w_out, norm_ffn, w_ffn_in, w_ffn_out, final_norm):
    batch, seq, d = x.shape
    depth = w_in.shape[0]
    assert d == D_MODEL and seq % SEQ_TILE == 0 and (batch * seq) % FFN_TILE == 0
    assert mem.shape == (batch, MEM_LEN, D_MODEL) and w_in.shape[2] == IN_WIDTH

    inv_freq = ROPE_BASE ** (-jnp.arange(0, RET_DIM, 2, dtype=F32) / RET_DIM)
    ang = positions.astype(F32)[:, None] * inv_freq[None, :]
    cos2 = jnp.concatenate([jnp.cos(ang), jnp.cos(ang)], axis=-1)
    sin2 = jnp.concatenate([-jnp.sin(ang), jnp.sin(ang)], axis=-1)
    decay, zeta, xi, gch = _retention_tables()

    to_bf16 = lambda w: w.astype(BF16)
    w_in_b, w_up_ret_b, w_up_pool_b, w_up_x_b, w_out_b = map(to_bf16, (w_in, w_up_ret, w_up_pool, w_up_x, w_out))
    w_ffn_in_b, w_ffn_out_b, w_mem_kv_b = map(to_bf16, (w_ffn_in, w_ffn_out, w_mem_kv))
    eye_g = jnp.eye(len(POOL_WINDOWS), dtype=F32)
    w_pool_bd = jnp.einsum('lgcd,gh->lgchd', w_pool_mix, eye_g).reshape(depth, POOL_WIDTH, POOL_WIDTH).astype(BF16)

    mk_bd, mv_bd = _memkv(mem, norm_mem, w_mem_kv_b)

    nmix = norm_mix.reshape(depth, 1, D_MODEL)
    nffn = norm_ffn.reshape(depth, 1, D_MODEL)
    pscale = pool_scale.reshape(depth, 1, POOL_WIDTH)
    fin = final_norm.reshape(1, D_MODEL)

    for l in range(depth):
        x = _mixer(x, cos2, sin2, l, nmix, w_in_b, w_up_ret_b, w_pool_bd, pscale, w_up_pool_b, mk_bd, mv_bd,
                   w_up_x_b, w_out_b, decay, zeta, xi, gch)
        x = _ffn(x.reshape(batch * seq, d), l, nffn, w_ffn_in_b, w_ffn_out_b, fin,
                 final_norm=(l == depth - 1)).reshape(batch, seq, d)
    return x
```

```python
import functools

import jax
import jax.numpy as jnp
from jax import lax
from jax.experimental import pallas as pl
from jax.experimental.pallas import tpu as pltpu

D_MODEL = 1024
MEM_LEN = 256
RET_HEADS = 4
RET_DIM = 128
RET_WIDTH = RET_HEADS * RET_DIM
RET_CHUNK = 128
ROPE_BASE = 10000.0
POOL_WINDOWS = (2, 4, 8, 16)
POOL_GROUP_DIM = 64
POOL_WIDTH = 256
POOL_HIST = 32
XA_HEADS = 4
XA_HEAD_DIM = 64
XA_WIDTH = 256
FFN_HIDDEN = 2816
EPS = 1e-6

OFF_Q = 0
OFF_K = OFF_Q + RET_WIDTH
OFF_V = OFF_K + RET_WIDTH
OFF_G = OFF_V + RET_WIDTH
OFF_U = OFF_G + RET_WIDTH
OFF_QX = OFF_U + POOL_WIDTH
OFF_GATES = OFF_QX + XA_WIDTH
IN_WIDTH = OFF_GATES + 3 * D_MODEL

SEQ_TILE = 512
FFN_TILE = 1024
SLAB = 256
VMEM_LIMIT_BYTES = 56 * 1024 * 1024

BF16 = jnp.bfloat16
F32 = jnp.float32


def _dot(a, b):
    return jnp.dot(a, b, preferred_element_type=F32)


def _rmsnorm_rows(x, g):
    return x * lax.rsqrt(jnp.mean(x * x, axis=-1, keepdims=True) + EPS) * g


def _zero_after_stores(stored):
    words = pltpu.bitcast(stored, jnp.uint32)
    acc = words[0:8]
    for r in range(8, words.shape[0], 8):
        acc = acc | words[r:r + 8]
    tile = acc[:, 0:128]
    for c in range(128, acc.shape[1], 128):
        tile = tile | acc[:, c:c + 128]
    return lax.shift_right_logical(lax.shift_right_logical(tile, jnp.uint32(16)), jnp.uint32(16))


NEXT_NORM_CHUNKS = ((0, 1), (2,), (3,), ())


def _memkv_body(mem_ref, g_ref, w_ref, mk_ref, mv_ref):
    m = _rmsnorm_rows(mem_ref[0], g_ref[0]).astype(BF16)
    kv = _dot(m, w_ref[0])
    keys_t = (kv[:, :XA_WIDTH] * (XA_HEAD_DIM ** -0.5)).T
    values = kv[:, XA_WIDTH:]
    head_shift = XA_HEAD_DIM.bit_length() - 1
    feature_head_k = lax.shift_right_logical(lax.broadcasted_iota(jnp.int32, (XA_WIDTH, MEM_LEN), 0), head_shift)
    feature_head_v = lax.shift_right_logical(lax.broadcasted_iota(jnp.int32, (MEM_LEN, XA_WIDTH), 1), head_shift)
    for hd in range(XA_HEADS):
        span = slice(hd * MEM_LEN, (hd + 1) * MEM_LEN)
        mk_ref[0, 0, :, span] = jnp.where(feature_head_k == hd, keys_t, 0.0).astype(BF16)
        mv_ref[0, 0, span, :] = jnp.where(feature_head_v == hd, values, 0.0).astype(BF16)


def _memkv(mem, norm_mem, w_mem_kv):
    depth = norm_mem.shape[0]
    batch = mem.shape[0]
    wide = XA_HEADS * MEM_LEN
    return pl.pallas_call(
        _memkv_body,
        grid=(depth, batch),
        in_specs=[
            pl.BlockSpec((1, MEM_LEN, D_MODEL), lambda l, b: (b, 0, 0)),
            pl.BlockSpec((1, 1, D_MODEL), lambda l, b: (l, 0, 0)),
            pl.BlockSpec((1, D_MODEL, 2 * XA_WIDTH), lambda l, b: (l, 0, 0)),
        ],
        out_specs=[pl.BlockSpec((1, 1, XA_WIDTH, wide), lambda l, b: (l, b, 0, 0)),
                   pl.BlockSpec((1, 1, wide, XA_WIDTH), lambda l, b: (l, b, 0, 0))],
        out_shape=[jax.ShapeDtypeStruct((depth, batch, XA_WIDTH, wide), BF16),
                   jax.ShapeDtypeStruct((depth, batch, wide, XA_WIDTH), BF16)],
        name="memkv",
    )(mem, norm_mem.reshape(depth, 1, D_MODEL), w_mem_kv)


def _mixer_body(x_ref, xn_ref, cos_ref, sin_ref, nmix_ref, w_in_ref, w_up_ret_ref, w_pool_ref, pscale_ref,
                w_up_pool_ref, mk_ref, mv_ref, w_up_x_ref, w_out_ref, decay_ref, zeta_ref, xi_ref,
                gch_ref, o_ref, state_scr, hist_scr, ext_u, ext_a, ext_b, h_scr, q_scr, qxi_scr, k_scr,
                kz_scr, v_scr, s_scr, y_scr, gsilu_scr, gate_scr, qx_scr, p_scr, att_scr, sums_scr, mixed_scr,
                merged_scr, mbf_scr):
    ts = x_ref.shape[1]
    n_chunks = ts // RET_CHUNK
    n_ext = POOL_HIST + ts
    j = pl.program_id(1)
    step = pl.program_id(0) * pl.num_programs(1) + j

    @pl.when(j == 0)
    def _start_of_sequence():
        state_scr[...] = jnp.zeros_like(state_scr)
        hist_scr[...] = jnp.zeros_like(hist_scr)

    @pl.when(step == 0)
    def _first_input_norm():
        h_scr[...] = _rmsnorm_rows(x_ref[0], nmix_ref[...]).astype(BF16)

    def project(col0):
        return _dot(h_scr[...], w_in_ref[:, col0:col0 + SLAB])

    cos2 = cos_ref[...]
    sin2 = sin_ref[...]

    def rope(t):
        return t * cos2 + pltpu.roll(t, RET_DIM // 2, 1) * sin2

    def store_rotated(slab, n, plain_scr, scaled_scr, row_scale_ref, mult):
        for i in range(SLAB // RET_DIM):
            hd = n * (SLAB // RET_DIM) + i
            lo = hd * RET_DIM
            t = rope(slab[:, i * RET_DIM:(i + 1) * RET_DIM])
            if mult is not None:
                t = t * mult
            plain_scr[:, lo:lo + RET_DIM] = t.astype(BF16)
            row_scale = row_scale_ref[hd]
            for c in range(n_chunks):
                rows = slice(c * RET_CHUNK, (c + 1) * RET_CHUNK)
                scaled_scr[rows, lo:lo + RET_DIM] = (t[rows] * row_scale).astype(BF16)

    for n in range(RET_WIDTH // SLAB):
        store_rotated(project(OFF_Q + n * SLAB), n, q_scr, qxi_scr, xi_ref, None)
    for n in range(RET_WIDTH // SLAB):
        store_rotated(project(OFF_K + n * SLAB), n, k_scr, kz_scr, zeta_ref, RET_DIM ** -0.5)
    for n in range(RET_WIDTH // SLAB):
        v_scr[:, n * SLAB:(n + 1) * SLAB] = project(OFF_V + n * SLAB).astype(BF16)

    def g_ret_slab(n):
        g = project(OFF_G + n * SLAB)
        gsilu_scr[:, n * SLAB:(n + 1) * SLAB] = g * jax.nn.sigmoid(g)

    def gate_slab(n):
        gate_scr[:, n * SLAB:(n + 1) * SLAB] = jax.nn.sigmoid(project(OFF_GATES + n * SLAB))

    def pool_in_slab():
        u = project(OFF_U)
        ext_u[0:POOL_HIST, :] = hist_scr[...]
        ext_u[POOL_HIST:n_ext, :] = u
        hist_scr[...] = u[ts - POOL_HIST:, :]

    def xattn_q_slab():
        qx_scr[...] = project(OFF_QX).astype(BF16)

    def retention_scores(c, hd):
        rows = slice(c * RET_CHUNK, (c + 1) * RET_CHUNK)
        cols = slice(hd * RET_DIM, (hd + 1) * RET_DIM)
        scores = lax.dot_general(q_scr[rows, cols], k_scr[rows, cols], (((1,), (1,)), ((), ())),
                                 preferred_element_type=F32)
        s_scr[rows, cols] = (scores * decay_ref[hd]).astype(BF16)

    def retention_out(c, hd):
        rows = slice(c * RET_CHUNK, (c + 1) * RET_CHUNK)
        cols = slice(hd * RET_DIM, (hd + 1) * RET_DIM)
        v = v_scr[rows, cols]
        state = state_scr[hd]
        lhs = jnp.concatenate([s_scr[rows, cols], qxi_scr[rows, cols]], axis=1)
        rhs = jnp.concatenate([v, state.astype(BF16)], axis=0)
        y_scr[rows, cols] = _dot(lhs, rhs)
        chunk_kv = lax.dot_general(kz_scr[rows, cols], v, (((0,), (0,)), ((), ())), preferred_element_type=F32)
        state_scr[hd] = state * gch_ref[hd] + chunk_kv

    def xattn_head(hd):
        cols = slice(hd * MEM_LEN, (hd + 1) * MEM_LEN)
        s = _dot(qx_scr[...], mk_ref[0, 0, :, cols])
        e = jnp.exp(s - jnp.max(s, axis=-1, keepdims=True))
        p_scr[:, cols] = (e * (1.0 / jnp.sum(e, axis=-1, keepdims=True))).astype(BF16)

    def xattn_values():
        att_scr[...] = _dot(p_scr[...], mv_ref[0, 0]).astype(BF16)

    def xattn_up():
        merged_scr[...] = gate_scr[:, 2 * D_MODEL:3 * D_MODEL] * _dot(att_scr[...], w_up_x_ref[...])

    def pool_level(k):
        src, dst = (ext_u, ext_a, ext_b, ext_a)[k], (ext_a, ext_b, ext_a, None)[k]
        lo, shift = 8 * (k + 1), 1 << k
        s = src[lo:n_ext, :] + src[lo - shift:n_ext - shift, :]
        if dst is not None:
            dst[lo:n_ext, :] = s
        lane = lax.broadcasted_iota(jnp.int32, (ts, POOL_WIDTH), 1)
        group = (lane >= k * POOL_GROUP_DIM) & (lane < (k + 1) * POOL_GROUP_DIM)
        tile_sum = s[POOL_HIST - lo:, :]
        sums_scr[...] = tile_sum if k == 0 else jnp.where(group, tile_sum, sums_scr[...])

    def pool_mix():
        lane = lax.broadcasted_iota(jnp.int32, (ts, POOL_WIDTH), 1)
        row = lax.broadcasted_iota(jnp.int32, (ts, POOL_WIDTH), 0)
        window = jnp.left_shift(2, lax.shift_right_logical(lane, 6))
        count = jnp.minimum(j * ts + row + 1, window).astype(F32)
        pooled = sums_scr[...] / count - ext_u[POOL_HIST:n_ext, :]
        mixed_scr[...] = (_dot(pooled.astype(BF16), w_pool_ref[...]) * pscale_ref[...]).astype(BF16)

    def pool_up():
        merged_scr[...] += gate_scr[:, D_MODEL:2 * D_MODEL] * _dot(mixed_scr[...], w_up_pool_ref[...])

    gates_per_branch = D_MODEL // SLAB
    part = functools.partial
    tasks = [part(g_ret_slab, n) for n in range(RET_WIDTH // SLAB)] + [pool_in_slab, xattn_q_slab]
    for hd in range(XA_HEADS):
        tasks += [part(xattn_head, hd), part(gate_slab, 2 * gates_per_branch + hd)]
    tasks += [xattn_values]
    for k in range(len(POOL_WINDOWS)):
        tasks += [part(pool_level, k), part(gate_slab, gates_per_branch + k)]
    tasks += [pool_mix, xattn_up, part(gate_slab, 0), pool_up]
    tasks += [part(gate_slab, n) for n in range(1, gates_per_branch)]

    units = [(c, hd) for c in range(n_chunks) for hd in range(RET_HEADS)]
    out_lag = 2
    assert len(tasks) >= len(units) + out_lag
    for i, task in enumerate(tasks):
        task()
        if i < len(units):
            retention_scores(*units[i])
        if out_lag <= i < len(units) + out_lag:
            retention_out(*units[i - out_lag])

    y_parts = []
    for hd in range(RET_HEADS):
        cols = slice(hd * RET_DIM, (hd + 1) * RET_DIM)
        yh = y_scr[:, cols]
        yh = yh * lax.rsqrt(jnp.mean(yh * yh, axis=-1, keepdims=True) + EPS)
        y_parts.append((yh * gsilu_scr[:, cols]).astype(BF16))
    y_ret = jnp.concatenate(y_parts, axis=1)
    for n in range(D_MODEL // SLAB):
        cols = slice(n * SLAB, (n + 1) * SLAB)
        b_ret = _dot(y_ret, w_up_ret_ref[:, cols])
        mbf_scr[:, cols] = (merged_scr[:, cols] + gate_scr[:, cols] * b_ret).astype(BF16)
    norm_rows = ts // sum(len(c) for c in NEXT_NORM_CHUNKS)
    for n, chunks in enumerate(NEXT_NORM_CHUNKS):
        zero = None
        for c in chunks:
            rows = slice(c * norm_rows, (c + 1) * norm_rows)
            h_scr[rows, :] = _rmsnorm_rows(xn_ref[0, rows, :], nmix_ref[...]).astype(BF16)
            z = _zero_after_stores(h_scr[rows, :])
            zero = z if zero is None else zero | z
        cols = slice(n * SLAB, (n + 1) * SLAB)
        y = x_ref[0, :, cols] + _dot(mbf_scr[...], w_out_ref[:, cols])
        o_ref[0, :, cols] = y
        if zero is not None:
            o_ref[0, 0:8, n * SLAB:n * SLAB + 128] = y[0:8, 0:128] + lax.bitcast_convert_type(zero, F32)


def _resident(shape, index_map):
    return pl.BlockSpec(shape, index_map, pipeline_mode=pl.Buffered(1))


def _mixer(x, cos2, sin2, layer, nmix, w_in, w_up_ret, w_pool_bd, pscale, w_up_pool, mk_bd, mv_bd,
           w_up_x, w_out, decay, zeta, xi, gch):
    batch, seq, _ = x.shape
    ts = SEQ_TILE
    l = layer
    n_tiles = seq // ts
    whole = lambda dims: (lambda b, j: (l,) + (0,) * dims)

    def next_tile(b, j):
        wraps = j + 1 == n_tiles
        return (jnp.where(wraps, jnp.minimum(b + 1, batch - 1), b), jnp.where(wraps, 0, j + 1), 0)

    const = lambda dims: (lambda b, j: (0,) * dims)
    in_specs = [
        pl.BlockSpec((1, ts, D_MODEL), lambda b, j: (b, j, 0)),
        pl.BlockSpec((1, ts, D_MODEL), next_tile),
        pl.BlockSpec((ts, RET_DIM), lambda b, j: (j, 0)),
        pl.BlockSpec((ts, RET_DIM), lambda b, j: (j, 0)),
        _resident((None, 1, D_MODEL), whole(2)),
        _resident((None, D_MODEL, IN_WIDTH), whole(2)),
        _resident((None, RET_WIDTH, D_MODEL), whole(2)),
        _resident((None, POOL_WIDTH, POOL_WIDTH), whole(2)),
        _resident((None, 1, POOL_WIDTH), whole(2)),
        _resident((None, POOL_WIDTH, D_MODEL), whole(2)),
        pl.BlockSpec((1, 1, XA_WIDTH, XA_HEADS * MEM_LEN), lambda b, j: (l, b, 0, 0)),
        pl.BlockSpec((1, 1, XA_HEADS * MEM_LEN, XA_WIDTH), lambda b, j: (l, b, 0, 0)),
        _resident((None, XA_WIDTH, D_MODEL), whole(2)),
        _resident((None, D_MODEL, D_MODEL), whole(2)),
        _resident((RET_HEADS, RET_CHUNK, RET_CHUNK), const(3)),
        _resident((RET_HEADS, RET_CHUNK, RET_DIM), const(3)),
        _resident((RET_HEADS, RET_CHUNK, RET_DIM), const(3)),
        _resident((RET_HEADS, 1, RET_DIM), const(3)),
    ]
    scratch = [
        pltpu.VMEM((RET_HEADS, RET_DIM, RET_DIM), F32),
        pltpu.VMEM((POOL_HIST, POOL_WIDTH), F32),
        pltpu.VMEM((POOL_HIST + ts, POOL_WIDTH), F32),
        pltpu.VMEM((POOL_HIST + ts, POOL_WIDTH), F32),
        pltpu.VMEM((POOL_HIST + ts, POOL_WIDTH), F32),
        pltpu.VMEM((ts, D_MODEL), BF16),
        pltpu.VMEM((ts, RET_WIDTH), BF16),
        pltpu.VMEM((ts, RET_WIDTH), BF16),
        pltpu.VMEM((ts, RET_WIDTH), BF16),
        pltpu.VMEM((ts, RET_WIDTH), BF16),
        pltpu.VMEM((ts, RET_WIDTH), BF16),
        pltpu.VMEM((ts, RET_WIDTH), BF16),
        pltpu.VMEM((ts, RET_WIDTH), F32),
        pltpu.VMEM((ts, RET_WIDTH), F32),
        pltpu.VMEM((ts, 3 * D_MODEL), F32),
        pltpu.VMEM((ts, XA_WIDTH), BF16),
        pltpu.VMEM((ts, XA_HEADS * MEM_LEN), BF16),
        pltpu.VMEM((ts, XA_WIDTH), BF16),
        pltpu.VMEM((ts, POOL_WIDTH), F32),
        pltpu.VMEM((ts, POOL_WIDTH), BF16),
        pltpu.VMEM((ts, D_MODEL), F32),
        pltpu.VMEM((ts, D_MODEL), BF16),
    ]
    return pl.pallas_call(
        _mixer_body,
        grid=(batch, seq // ts),
        in_specs=in_specs,
        out_specs=pl.BlockSpec((1, ts, D_MODEL), lambda b, j: (b, j, 0)),
        out_shape=jax.ShapeDtypeStruct(x.shape, x.dtype),
        scratch_shapes=scratch,
        compiler_params=pltpu.CompilerParams(
            dimension_semantics=("arbitrary", "arbitrary"), vmem_limit_bytes=VMEM_LIMIT_BYTES),
        name="mixer",
    )(x, x, cos2, sin2, nmix, w_in, w_up_ret, w_pool_bd, pscale, w_up_pool, mk_bd, mv_bd, w_up_x, w_out,
      decay, zeta, xi, gch)


def _ffn_body(x_ref, xn_ref, g_ref, wa_ref, wb_ref, wo_ref, fin_ref, o_ref, h_scr, hid_scr, *, final_norm):
    ts = x_ref.shape[0]

    @pl.when(pl.program_id(0) == 0)
    def _first_input_norm():
        h_scr[...] = _rmsnorm_rows(x_ref[...], g_ref[...]).astype(BF16)

    for n in range(FFN_HIDDEN // SLAB):
        cols = slice(n * SLAB, (n + 1) * SLAB)
        a = _dot(h_scr[...], wa_ref[:, cols])
        b = _dot(h_scr[...], wb_ref[:, cols])
        hid_scr[:, cols] = (a * jax.nn.sigmoid(a) * b).astype(BF16)

    norm_rows = ts // sum(len(c) for c in NEXT_NORM_CHUNKS)
    for n, chunks in enumerate(NEXT_NORM_CHUNKS):
        zero = None
        for c in chunks:
            rows = slice(c * norm_rows, (c + 1) * norm_rows)
            h_scr[rows, :] = _rmsnorm_rows(xn_ref[rows, :], g_ref[...]).astype(BF16)
            z = _zero_after_stores(h_scr[rows, :])
            zero = z if zero is None else zero | z
        cols = slice(n * SLAB, (n + 1) * SLAB)
        y = x_ref[:, cols] + _dot(hid_scr[...], wo_ref[:, cols])
        o_ref[:, cols] = y
        if zero is not None:
            o_ref[0:8, n * SLAB:n * SLAB + 128] = y[0:8, 0:128] + lax.bitcast_convert_type(zero, F32)
    if final_norm:
        o_ref[...] = _rmsnorm_rows(o_ref[...], fin_ref[...])


def _ffn(x, layer, norm_ffn, w_ffn_in, w_ffn_out, final_gain, final_norm):
    tokens = x.shape[0]
    ts = FFN_TILE
    l = layer
    last = tokens // ts - 1
    return pl.pallas_call(
        functools.partial(_ffn_body, final_norm=final_norm),
        grid=(tokens // ts,),
        in_specs=[
            pl.BlockSpec((ts, D_MODEL), lambda i: (i, 0)),
            pl.BlockSpec((ts, D_MODEL), lambda i: (jnp.minimum(i + 1, last), 0)),
            _resident((None, 1, D_MODEL), lambda i: (l, 0, 0)),
            _resident((None, D_MODEL, FFN_HIDDEN), lambda i: (l, 0, 0)),
            _resident((None, D_MODEL, FFN_HIDDEN), lambda i: (l, 0, 1)),
            _resident((None, FFN_HIDDEN, D_MODEL), lambda i: (l, 0, 0)),
            _resident((1, D_MODEL), lambda i: (0, 0)),
        ],
        out_specs=pl.BlockSpec((ts, D_MODEL), lambda i: (i, 0)),
        out_shape=jax.ShapeDtypeStruct(x.shape, x.dtype),
        scratch_shapes=[pltpu.VMEM((ts, D_MODEL), BF16), pltpu.VMEM((ts, FFN_HIDDEN), BF16)],
        compiler_params=pltpu.CompilerParams(
            dimension_semantics=("arbitrary",), vmem_limit_bytes=VMEM_LIMIT_BYTES),
        name="ffn_final" if final_norm else "ffn",
    )(x, x, norm_ffn, w_ffn_in, w_ffn_in, w_ffn_out, final_gain)


def _retention_tables():
    heads = jnp.arange(RET_HEADS, dtype=F32)
    log_g = jnp.log(1.0 - jnp.exp2(-5.0 - heads))
    idx = jnp.arange(RET_CHUNK, dtype=F32)
    diff = idx[:, None] - idx[None, :]
    decay = jnp.where(diff[None] >= 0, jnp.exp(jnp.maximum(diff, 0.0)[None] * log_g[:, None, None]), 0.0)
    zeta = jnp.exp((RET_CHUNK - 1.0 - idx)[None, :] * log_g[:, None])
    xi = jnp.exp((idx + 1.0)[None, :] * log_g[:, None])
    gch = jnp.exp(RET_CHUNK * log_g)
    rows = lambda t: jnp.broadcast_to(t[:, :, None], (RET_HEADS, RET_CHUNK, RET_DIM))
    return decay, rows(zeta), rows(xi), jnp.broadcast_to(gch[:, None, None], (RET_HEADS, 1, RET_DIM))


def kernel(x, mem, positions, norm_mix, w_in, w_up_ret, w_pool_mix, pool_scale, w_up_pool, norm_mem,
           w_mem_kv, w_up_x, w_out, norm_ffn, w_ffn_in, w_ffn_out, final_norm):
    batch, seq, d = x.shape
    depth = w_in.shape[0]
    assert d == D_MODEL and seq % SEQ_TILE == 0 and (batch * seq) % FFN_TILE == 0
    assert mem.shape == (batch, MEM_LEN, D_MODEL) and w_in.shape[2] == IN_WIDTH

    inv_freq = ROPE_BASE ** (-jnp.arange(0, RET_DIM, 2, dtype=F32) / RET_DIM)
    ang = positions.astype(F32)[:, None] * inv_freq[None, :]
    cos2 = jnp.concatenate([jnp.cos(ang), jnp.cos(ang)], axis=-1)
    sin2 = jnp.concatenate([-jnp.sin(ang), jnp.sin(ang)], axis=-1)
    decay, zeta, xi, gch = _retention_tables()

    to_bf16 = lambda w: w.astype(BF16)
    w_in_b, w_up_ret_b, w_up_pool_b, w_up_x_b, w_out_b = map(to_bf16, (w_in, w_up_ret, w_up_pool, w_up_x, w_out))
    w_ffn_in_b, w_ffn_out_b, w_mem_kv_b = map(to_bf16, (w_ffn_in, w_ffn_out, w_mem_kv))
    eye_g = jnp.eye(len(POOL_WINDOWS), dtype=F32)
    w_pool_bd = jnp.einsum('lgcd,gh->lgchd', w_pool_mix, eye_g).reshape(depth, POOL_WIDTH, POOL_WIDTH).astype(BF16)

    mk_bd, mv_bd = _memkv(mem, norm_mem, w_mem_kv_b)

    nmix = norm_mix.reshape(depth, 1, D_MODEL)
    nffn = norm_ffn.reshape(depth, 1, D_MODEL)
    pscale = pool_scale.reshape(depth, 1, POOL_WIDTH)
    fin = final_norm.reshape(1, D_MODEL)

    for l in range(depth):
        x = _mixer(x, cos2, sin2, l, nmix, w_in_b, w_up_ret_b, w_pool_bd, pscale, w_up_pool_b, mk_bd, mv_bd,
                   w_up_x_b, w_out_b, decay, zeta, xi, gch)
        x = _ffn(x.reshape(batch * seq, d), l, nffn, w_ffn_in_b, w_ffn_out_b, fin,
                 final_norm=(l == depth - 1)).reshape(batch, seq, d)
    return x
```

```python
import functools

import jax
import jax.numpy as jnp
from jax import lax
from jax.experimental import pallas as pl
from jax.experimental.pallas import tpu as pltpu

D_MODEL = 1024
MEM_LEN = 256
RET_HEADS = 4
RET_DIM = 128
RET_WIDTH = RET_HEADS * RET_DIM
RET_CHUNK = 128
ROPE_BASE = 10000.0
POOL_WINDOWS = (2, 4, 8, 16)
POOL_GROUP_DIM = 64
POOL_WIDTH = 256
POOL_HIST = 32
XA_HEADS = 4
XA_HEAD_DIM = 64
XA_WIDTH = 256
FFN_HIDDEN = 2816
EPS = 1e-6

OFF_Q = 0
OFF_K = OFF_Q + RET_WIDTH
OFF_V = OFF_K + RET_WIDTH
OFF_G = OFF_V + RET_WIDTH
OFF_U = OFF_G + RET_WIDTH
OFF_QX = OFF_U + POOL_WIDTH
OFF_GATES = OFF_QX + XA_WIDTH
IN_WIDTH = OFF_GATES + 3 * D_MODEL

SEQ_TILE = 512
FFN_TILE = 1024
SLAB = 256
VMEM_LIMIT_BYTES = 56 * 1024 * 1024

BF16 = jnp.bfloat16
F32 = jnp.float32


def _dot(a, b):
    return jnp.dot(a, b, preferred_element_type=F32)


def _rmsnorm_rows(x, g):
    return x * lax.rsqrt(jnp.mean(x * x, axis=-1, keepdims=True) + EPS) * g


def _memkv_body(mem_ref, g_ref, w_ref, mk_ref, mv_ref):
    m = _rmsnorm_rows(mem_ref[0], g_ref[0]).astype(BF16)
    kv = _dot(m, w_ref[0])
    keys_t = (kv[:, :XA_WIDTH] * (XA_HEAD_DIM ** -0.5)).T
    values = kv[:, XA_WIDTH:]
    head_shift = XA_HEAD_DIM.bit_length() - 1
    feature_head_k = lax.shift_right_logical(lax.broadcasted_iota(jnp.int32, (XA_WIDTH, MEM_LEN), 0), head_shift)
    feature_head_v = lax.shift_right_logical(lax.broadcasted_iota(jnp.int32, (MEM_LEN, XA_WIDTH), 1), head_shift)
    for hd in range(XA_HEADS):
        span = slice(hd * MEM_LEN, (hd + 1) * MEM_LEN)
        mk_ref[0, 0, :, span] = jnp.where(feature_head_k == hd, keys_t, 0.0).astype(BF16)
        mv_ref[0, 0, span, :] = jnp.where(feature_head_v == hd, values, 0.0).astype(BF16)


def _memkv(mem, norm_mem, w_mem_kv):
    depth = norm_mem.shape[0]
    batch = mem.shape[0]
    wide = XA_HEADS * MEM_LEN
    return pl.pallas_call(
        _memkv_body,
        grid=(depth, batch),
        in_specs=[
            pl.BlockSpec((1, MEM_LEN, D_MODEL), lambda l, b: (b, 0, 0)),
            pl.BlockSpec((1, 1, D_MODEL), lambda l, b: (l, 0, 0)),
            pl.BlockSpec((1, D_MODEL, 2 * XA_WIDTH), lambda l, b: (l, 0, 0)),
        ],
        out_specs=[pl.BlockSpec((1, 1, XA_WIDTH, wide), lambda l, b: (l, b, 0, 0)),
                   pl.BlockSpec((1, 1, wide, XA_WIDTH), lambda l, b: (l, b, 0, 0))],
        out_shape=[jax.ShapeDtypeStruct((depth, batch, XA_WIDTH, wide), BF16),
                   jax.ShapeDtypeStruct((depth, batch, wide, XA_WIDTH), BF16)],
        name="memkv",
    )(mem, norm_mem.reshape(depth, 1, D_MODEL), w_mem_kv)


def _mixer_body(x_ref, cos_ref, sin_ref, nmix_ref, w_in_ref, w_up_ret_ref, w_pool_ref, pscale_ref,
                w_up_pool_ref, mk_ref, mv_ref, w_up_x_ref, w_out_ref, decay_ref, zeta_ref, xi_ref,
                gch_ref, o_ref, state_scr, hist_scr, ext_u, ext_a, ext_b, h_scr, q_scr, qxi_scr, k_scr,
                kz_scr, v_scr, s_scr, y_scr, gsilu_scr, gate_scr, qx_scr, p_scr, att_scr, sums_scr, mixed_scr,
                merged_scr, mbf_scr):
    ts = x_ref.shape[1]
    n_chunks = ts // RET_CHUNK
    n_ext = POOL_HIST + ts
    j = pl.program_id(1)

    @pl.when(j == 0)
    def _start_of_sequence():
        state_scr[...] = jnp.zeros_like(state_scr)
        hist_scr[...] = jnp.zeros_like(hist_scr)

    h_scr[...] = _rmsnorm_rows(x_ref[0], nmix_ref[...]).astype(BF16)

    def project(col0):
        return _dot(h_scr[...], w_in_ref[:, col0:col0 + SLAB])

    tile_rows = pl.ds(pl.multiple_of(j * ts, ts), ts)
    cos2 = cos_ref[tile_rows, :]
    sin2 = sin_ref[tile_rows, :]

    def rope(t):
        return t * cos2 + pltpu.roll(t, RET_DIM // 2, 1) * sin2

    def store_rotated(slab, n, plain_scr, scaled_scr, row_scale_ref, mult):
        for i in range(SLAB // RET_DIM):
            hd = n * (SLAB // RET_DIM) + i
            lo = hd * RET_DIM
            t = rope(slab[:, i * RET_DIM:(i + 1) * RET_DIM])
            if mult is not None:
                t = t * mult
            plain_scr[:, lo:lo + RET_DIM] = t.astype(BF16)
            row_scale = row_scale_ref[hd]
            for c in range(n_chunks):
                rows = slice(c * RET_CHUNK, (c + 1) * RET_CHUNK)
                scaled_scr[rows, lo:lo + RET_DIM] = (t[rows] * row_scale).astype(BF16)

    for n in range(RET_WIDTH // SLAB):
        store_rotated(project(OFF_Q + n * SLAB), n, q_scr, qxi_scr, xi_ref, None)
    for n in range(RET_WIDTH // SLAB):
        store_rotated(project(OFF_K + n * SLAB), n, k_scr, kz_scr, zeta_ref, RET_DIM ** -0.5)
    for n in range(RET_WIDTH // SLAB):
        v_scr[:, n * SLAB:(n + 1) * SLAB] = project(OFF_V + n * SLAB).astype(BF16)

    def g_ret_slab(n):
        g = project(OFF_G + n * SLAB)
        gsilu_scr[:, n * SLAB:(n + 1) * SLAB] = g * jax.nn.sigmoid(g)

    def gate_slab(n):
        gate_scr[:, n * SLAB:(n + 1) * SLAB] = jax.nn.sigmoid(project(OFF_GATES + n * SLAB))

    def pool_in_slab():
        u = project(OFF_U)
        ext_u[0:POOL_HIST, :] = hist_scr[...]
        ext_u[POOL_HIST:n_ext, :] = u
        hist_scr[...] = u[ts - POOL_HIST:, :]

    def xattn_q_slab():
        qx_scr[...] = project(OFF_QX).astype(BF16)

    def retention_scores(c, hd):
        rows = slice(c * RET_CHUNK, (c + 1) * RET_CHUNK)
        cols = slice(hd * RET_DIM, (hd + 1) * RET_DIM)
        scores = lax.dot_general(q_scr[rows, cols], k_scr[rows, cols], (((1,), (1,)), ((), ())),
                                 preferred_element_type=F32)
        s_scr[rows, cols] = (scores * decay_ref[hd]).astype(BF16)

    def retention_out(c, hd):
        rows = slice(c * RET_CHUNK, (c + 1) * RET_CHUNK)
        cols = slice(hd * RET_DIM, (hd + 1) * RET_DIM)
        v = v_scr[rows, cols]
        state = state_scr[hd]
        lhs = jnp.concatenate([s_scr[rows, cols], qxi_scr[rows, cols]], axis=1)
        rhs = jnp.concatenate([v, state.astype(BF16)], axis=0)
        y_scr[rows, cols] = _dot(lhs, rhs)
        chunk_kv = lax.dot_general(kz_scr[rows, cols], v, (((0,), (0,)), ((), ())), preferred_element_type=F32)
        state_scr[hd] = state * gch_ref[hd] + chunk_kv

    def xattn_head(hd):
        cols = slice(hd * MEM_LEN, (hd + 1) * MEM_LEN)
        s = _dot(qx_scr[...], mk_ref[0, 0, :, cols])
        e = jnp.exp(s - jnp.max(s, axis=-1, keepdims=True))
        p_scr[:, cols] = (e * (1.0 / jnp.sum(e, axis=-1, keepdims=True))).astype(BF16)

    def xattn_values():
        att_scr[...] = _dot(p_scr[...], mv_ref[0, 0]).astype(BF16)

    def xattn_up():
        merged_scr[...] = gate_scr[:, 2 * D_MODEL:3 * D_MODEL] * _dot(att_scr[...], w_up_x_ref[...])

    def pool_level(k):
        src, dst = (ext_u, ext_a, ext_b, ext_a)[k], (ext_a, ext_b, ext_a, None)[k]
        lo, shift = 8 * (k + 1), 1 << k
        s = src[lo:n_ext, :] + src[lo - shift:n_ext - shift, :]
        if dst is not None:
            dst[lo:n_ext, :] = s
        lane = lax.broadcasted_iota(jnp.int32, (ts, POOL_WIDTH), 1)
        group = (lane >= k * POOL_GROUP_DIM) & (lane < (k + 1) * POOL_GROUP_DIM)
        tile_sum = s[POOL_HIST - lo:, :]
        sums_scr[...] = tile_sum if k == 0 else jnp.where(group, tile_sum, sums_scr[...])

    def pool_mix():
        lane = lax.broadcasted_iota(jnp.int32, (ts, POOL_WIDTH), 1)
        row = lax.broadcasted_iota(jnp.int32, (ts, POOL_WIDTH), 0)
        window = jnp.left_shift(2, lax.shift_right_logical(lane, 6))
        count = jnp.minimum(j * ts + row + 1, window).astype(F32)
        pooled = sums_scr[...] / count - ext_u[POOL_HIST:n_ext, :]
        mixed_scr[...] = (_dot(pooled.astype(BF16), w_pool_ref[...]) * pscale_ref[...]).astype(BF16)

    def pool_up():
        merged_scr[...] += gate_scr[:, D_MODEL:2 * D_MODEL] * _dot(mixed_scr[...], w_up_pool_ref[...])

    gates_per_branch = D_MODEL // SLAB
    part = functools.partial
    tasks = [part(g_ret_slab, n) for n in range(RET_WIDTH // SLAB)] + [pool_in_slab, xattn_q_slab]
    for hd in range(XA_HEADS):
        tasks += [part(xattn_head, hd), part(gate_slab, 2 * gates_per_branch + hd)]
    tasks += [xattn_values]
    for k in range(len(POOL_WINDOWS)):
        tasks += [part(pool_level, k), part(gate_slab, gates_per_branch + k)]
    tasks += [pool_mix, xattn_up, part(gate_slab, 0), pool_up]
    tasks += [part(gate_slab, n) for n in range(1, gates_per_branch)]

    units = [(c, hd) for c in range(n_chunks) for hd in range(RET_HEADS)]
    out_lag = 2
    assert len(tasks) >= len(units) + out_lag
    for i, task in enumerate(tasks):
        task()
        if i < len(units):
            retention_scores(*units[i])
        if out_lag <= i < len(units) + out_lag:
            retention_out(*units[i - out_lag])

    y_parts = []
    for hd in range(RET_HEADS):
        cols = slice(hd * RET_DIM, (hd + 1) * RET_DIM)
        yh = y_scr[:, cols]
        yh = yh * lax.rsqrt(jnp.mean(yh * yh, axis=-1, keepdims=True) + EPS)
        y_parts.append((yh * gsilu_scr[:, cols]).astype(BF16))
    y_ret = jnp.concatenate(y_parts, axis=1)
    for n in range(D_MODEL // SLAB):
        cols = slice(n * SLAB, (n + 1) * SLAB)
        b_ret = _dot(y_ret, w_up_ret_ref[:, cols])
        mbf_scr[:, cols] = (merged_scr[:, cols] + gate_scr[:, cols] * b_ret).astype(BF16)
    for n in range(D_MODEL // SLAB):
        cols = slice(n * SLAB, (n + 1) * SLAB)
        o_ref[0, :, cols] = x_ref[0, :, cols] + _dot(mbf_scr[...], w_out_ref[:, cols])


def _resident(shape, index_map):
    return pl.BlockSpec(shape, index_map, pipeline_mode=pl.Buffered(1))


def _mixer(x, cos2, sin2, layer, nmix, w_in, w_up_ret, w_pool_bd, pscale, w_up_pool, mk_bd, mv_bd,
           w_up_x, w_out, decay, zeta, xi, gch):
    batch, seq, _ = x.shape
    ts = SEQ_TILE
    l = layer
    whole = lambda dims: (lambda b, j: (l,) + (0,) * dims)
    const = lambda dims: (lambda b, j: (0,) * dims)
    in_specs = [
        pl.BlockSpec((1, ts, D_MODEL), lambda b, j: (b, j, 0)),
        _resident((seq, RET_DIM), const(2)),
        _resident((seq, RET_DIM), const(2)),
        _resident((None, 1, D_MODEL), whole(2)),
        _resident((None, D_MODEL, IN_WIDTH), whole(2)),
        _resident((None, RET_WIDTH, D_MODEL), whole(2)),
        _resident((None, POOL_WIDTH, POOL_WIDTH), whole(2)),
        _resident((None, 1, POOL_WIDTH), whole(2)),
        _resident((None, POOL_WIDTH, D_MODEL), whole(2)),
        pl.BlockSpec((1, 1, XA_WIDTH, XA_HEADS * MEM_LEN), lambda b, j: (l, b, 0, 0)),
        pl.BlockSpec((1, 1, XA_HEADS * MEM_LEN, XA_WIDTH), lambda b, j: (l, b, 0, 0)),
        _resident((None, XA_WIDTH, D_MODEL), whole(2)),
        _resident((None, D_MODEL, D_MODEL), whole(2)),
        _resident((RET_HEADS, RET_CHUNK, RET_CHUNK), const(3)),
        _resident((RET_HEADS, RET_CHUNK, RET_DIM), const(3)),
        _resident((RET_HEADS, RET_CHUNK, RET_DIM), const(3)),
        _resident((RET_HEADS, 1, RET_DIM), const(3)),
    ]
    scratch = [
        pltpu.VMEM((RET_HEADS, RET_DIM, RET_DIM), F32),
        pltpu.VMEM((POOL_HIST, POOL_WIDTH), F32),
        pltpu.VMEM((POOL_HIST + ts, POOL_WIDTH), F32),
        pltpu.VMEM((POOL_HIST + ts, POOL_WIDTH), F32),
        pltpu.VMEM((POOL_HIST + ts, POOL_WIDTH), F32),
        pltpu.VMEM((ts, D_MODEL), BF16),
        pltpu.VMEM((ts, RET_WIDTH), BF16),
        pltpu.VMEM((ts, RET_WIDTH), BF16),
        pltpu.VMEM((ts, RET_WIDTH), BF16),
        pltpu.VMEM((ts, RET_WIDTH), BF16),
        pltpu.VMEM((ts, RET_WIDTH), BF16),
        pltpu.VMEM((ts, RET_WIDTH), BF16),
        pltpu.VMEM((ts, RET_WIDTH), F32),
        pltpu.VMEM((ts, RET_WIDTH), F32),
        pltpu.VMEM((ts, 3 * D_MODEL), F32),
        pltpu.VMEM((ts, XA_WIDTH), BF16),
        pltpu.VMEM((ts, XA_HEADS * MEM_LEN), BF16),
        pltpu.VMEM((ts, XA_WIDTH), BF16),
        pltpu.VMEM((ts, POOL_WIDTH), F32),
        pltpu.VMEM((ts, POOL_WIDTH), BF16),
        pltpu.VMEM((ts, D_MODEL), F32),
        pltpu.VMEM((ts, D_MODEL), BF16),
    ]
    return pl.pallas_call(
        _mixer_body,
        grid=(batch, seq // ts),
        in_specs=in_specs,
        out_specs=pl.BlockSpec((1, ts, D_MODEL), lambda b, j: (b, j, 0)),
        out_shape=jax.ShapeDtypeStruct(x.shape, x.dtype),
        scratch_shapes=scratch,
        compiler_params=pltpu.CompilerParams(
            dimension_semantics=("arbitrary", "arbitrary"), vmem_limit_bytes=VMEM_LIMIT_BYTES),
        name="mixer",
    )(x, cos2, sin2, nmix, w_in, w_up_ret, w_pool_bd, pscale, w_up_pool, mk_bd, mv_bd, w_up_x, w_out,
      decay, zeta, xi, gch)


def _ffn_body(x_ref, g_ref, wa_ref, wb_ref, wo_ref, fin_ref, o_ref, hid_scr, *, final_norm):
    x = x_ref[...]
    h = _rmsnorm_rows(x, g_ref[...]).astype(BF16)
    for n in range(FFN_HIDDEN // SLAB):
        cols = slice(n * SLAB, (n + 1) * SLAB)
        a = _dot(h, wa_ref[:, cols])
        b = _dot(h, wb_ref[:, cols])
        hid_scr[:, cols] = (a * jax.nn.sigmoid(a) * b).astype(BF16)
    y = x + _dot(hid_scr[...], wo_ref[...])
    if final_norm:
        y = _rmsnorm_rows(y, fin_ref[...])
    o_ref[...] = y


def _ffn(x, layer, norm_ffn, w_ffn_in, w_ffn_out, final_gain, final_norm):
    tokens = x.shape[0]
    ts = FFN_TILE
    l = layer
    return pl.pallas_call(
        functools.partial(_ffn_body, final_norm=final_norm),
        grid=(tokens // ts,),
        in_specs=[
            pl.BlockSpec((ts, D_MODEL), lambda i: (i, 0)),
            _resident((None, 1, D_MODEL), lambda i: (l, 0, 0)),
            _resident((None, D_MODEL, FFN_HIDDEN), lambda i: (l, 0, 0)),
            _resident((None, D_MODEL, FFN_HIDDEN), lambda i: (l, 0, 1)),
            _resident((None, FFN_HIDDEN, D_MODEL), lambda i: (l, 0, 0)),
            _resident((1, D_MODEL), lambda i: (0, 0)),
        ],
        out_specs=pl.BlockSpec((ts, D_MODEL), lambda i: (i, 0)),
        out_shape=jax.ShapeDtypeStruct(x.shape, x.dtype),
        scratch_shapes=[pltpu.VMEM((ts, FFN_HIDDEN), BF16)],
        compiler_params=pltpu.CompilerParams(
            dimension_semantics=("arbitrary",), vmem_limit_bytes=VMEM_LIMIT_BYTES),
        name="ffn_final" if final_norm else "ffn",
    )(x, norm_ffn, w_ffn_in, w_ffn_in, w_ffn_out, final_gain)


def _retention_tables():
    heads = jnp.arange(RET_HEADS, dtype=F32)
    log_g = jnp.log(1.0 - jnp.exp2(-5.0 - heads))
    idx = jnp.arange(RET_CHUNK, dtype=F32)
    diff = idx[:, None] - idx[None, :]
    decay = jnp.where(diff[None] >= 0, jnp.exp(jnp.maximum(diff, 0.0)[None] * log_g[:, None, None]), 0.0)
    zeta = jnp.exp((RET_CHUNK - 1.0 - idx)[None, :] * log_g[:, None])
    xi = jnp.exp((idx + 1.0)[None, :] * log_g[:, None])
    gch = jnp.exp(RET_CHUNK * log_g)
    rows = lambda t: jnp.broadcast_to(t[:, :, None], (RET_HEADS, RET_CHUNK, RET_DIM))
    return decay, rows(zeta), rows(xi), jnp.broadcast_to(gch[:, None, None], (RET_HEADS, 1, RET_DIM))


def kernel(x, mem, positions, norm_mix, w_in, w_up_ret, w_pool_mix, pool_scale, w_up_pool, norm_mem,
           w_mem_kv, w_up_x, w_out, norm_ffn, w_ffn_in, w_ffn_out, final_norm):
    batch, seq, d = x.shape
    depth = w_in.shape[0]
    assert d == D_MODEL and seq % SEQ_TILE == 0 and (batch * seq) % FFN_TILE == 0
    assert mem.shape == (batch, MEM_LEN, D_MODEL) and w_in.shape[2] == IN_WIDTH

    inv_freq = ROPE_BASE ** (-jnp.arange(0, RET_DIM, 2, dtype=F32) / RET_DIM)
    ang = positions.astype(F32)[:, None] * inv_freq[None, :]
    cos2 = jnp.concatenate([jnp.cos(ang), jnp.cos(ang)], axis=-1)
    sin2 = jnp.concatenate([-jnp.sin(ang), jnp.sin(ang)], axis=-1)
    decay, zeta, xi, gch = _retention_tables()

    to_bf16 = lambda w: w.astype(BF16)
    w_in_b, w_up_ret_b, w_up_pool_b, w_up_x_b, w_out_b = map(to_bf16, (w_in, w_up_ret, w_up_pool, w_up_x, w_out))
    w_ffn_in_b, w_ffn_out_b, w_mem_kv_b = map(to_bf16, (w_ffn_in, w_ffn_out, w_mem_kv))
    eye_g = jnp.eye(len(POOL_WINDOWS), dtype=F32)
    w_pool_bd = jnp.einsum('lgcd,gh->lgchd', w_pool_mix, eye_g).reshape(depth, POOL_WIDTH, POOL_WIDTH).astype(BF16)

    mk_bd, mv_bd = _memkv(mem, norm_mem, w_mem_kv_b)

    nmix = norm_mix.reshape(depth, 1, D_MODEL)
    nffn = norm_ffn.reshape(depth, 1, D_MODEL)
    pscale = pool_scale.reshape(depth, 1, POOL_WIDTH)
    fin = final_norm.reshape(1, D_MODEL)

    for l in range(depth):
        x = _mixer(x, cos2, sin2, l, nmix, w_in_b, w_up_ret_b, w_pool_bd, pscale, w_up_pool_b, mk_bd, mv_bd,
                   w_up_x_b, w_out_b, decay, zeta, xi, gch)
        x = _ffn(x.reshape(batch * seq, d), l, nffn, w_ffn_in_b, w_ffn_out_b, fin,
                 final_norm=(l == depth - 1)).reshape(batch, seq, d)
    return x
```

```python
import functools

import jax
import jax.numpy as jnp
from jax import lax
from jax.experimental import pallas as pl
from jax.experimental.pallas import tpu as pltpu

D_MODEL = 1024
MEM_LEN = 256
RET_HEADS = 4
RET_DIM = 128
RET_WIDTH = RET_HEADS * RET_DIM
RET_CHUNK = 128
ROPE_BASE = 10000.0
POOL_WINDOWS = (2, 4, 8, 16)
POOL_GROUP_DIM = 64
POOL_WIDTH = 256
POOL_HIST = 32
XA_HEADS = 4
XA_HEAD_DIM = 64
XA_WIDTH = 256
FFN_HIDDEN = 2816
EPS = 1e-6

OFF_Q = 0
OFF_K = OFF_Q + RET_WIDTH
OFF_V = OFF_K + RET_WIDTH
OFF_G = OFF_V + RET_WIDTH
OFF_U = OFF_G + RET_WIDTH
OFF_QX = OFF_U + POOL_WIDTH
OFF_GATES = OFF_QX + XA_WIDTH
IN_WIDTH = OFF_GATES + 3 * D_MODEL

SEQ_TILE = 512
FFN_TILE = 1024
SLAB = 256
VMEM_LIMIT_BYTES = 56 * 1024 * 1024

BF16 = jnp.bfloat16
F32 = jnp.float32


def _dot(a, b):
    return jnp.dot(a, b, preferred_element_type=F32)


def _rmsnorm_rows(x, g):
    return x * lax.rsqrt(jnp.mean(x * x, axis=-1, keepdims=True) + EPS) * g


def _split_rmsnorm(x, g):
    inv_rms = lax.rsqrt(jnp.mean(x * x, axis=-1, keepdims=True) + EPS)
    return (x * g).astype(BF16), jnp.broadcast_to(inv_rms, (x.shape[0], SLAB))


def _memkv_body(mem_ref, g_ref, w_ref, mk_ref, mv_ref):
    m = _rmsnorm_rows(mem_ref[0], g_ref[0]).astype(BF16)
    kv = _dot(m, w_ref[0])
    keys_t = (kv[:, :XA_WIDTH] * (XA_HEAD_DIM ** -0.5)).T
    values = kv[:, XA_WIDTH:]
    head_shift = XA_HEAD_DIM.bit_length() - 1
    feature_head_k = lax.shift_right_logical(lax.broadcasted_iota(jnp.int32, (XA_WIDTH, MEM_LEN), 0), head_shift)
    feature_head_v = lax.shift_right_logical(lax.broadcasted_iota(jnp.int32, (MEM_LEN, XA_WIDTH), 1), head_shift)
    for hd in range(XA_HEADS):
        span = slice(hd * MEM_LEN, (hd + 1) * MEM_LEN)
        mk_ref[0, 0, :, span] = jnp.where(feature_head_k == hd, keys_t, 0.0).astype(BF16)
        mv_ref[0, 0, span, :] = jnp.where(feature_head_v == hd, values, 0.0).astype(BF16)


def _memkv(mem, norm_mem, w_mem_kv):
    depth = norm_mem.shape[0]
    batch = mem.shape[0]
    wide = XA_HEADS * MEM_LEN
    return pl.pallas_call(
        _memkv_body,
        grid=(depth, batch),
        in_specs=[
            pl.BlockSpec((1, MEM_LEN, D_MODEL), lambda l, b: (b, 0, 0)),
            pl.BlockSpec((1, 1, D_MODEL), lambda l, b: (l, 0, 0)),
            pl.BlockSpec((1, D_MODEL, 2 * XA_WIDTH), lambda l, b: (l, 0, 0)),
        ],
        out_specs=[pl.BlockSpec((1, 1, XA_WIDTH, wide), lambda l, b: (l, b, 0, 0)),
                   pl.BlockSpec((1, 1, wide, XA_WIDTH), lambda l, b: (l, b, 0, 0))],
        out_shape=[jax.ShapeDtypeStruct((depth, batch, XA_WIDTH, wide), BF16),
                   jax.ShapeDtypeStruct((depth, batch, wide, XA_WIDTH), BF16)],
        name="memkv",
    )(mem, norm_mem.reshape(depth, 1, D_MODEL), w_mem_kv)


def _mixer_body(x_ref, cos_ref, sin_ref, nmix_ref, w_in_ref, w_up_ret_ref, w_pool_ref, pscale_ref,
                w_up_pool_ref, mk_ref, mv_ref, w_up_x_ref, w_out_ref, decay_ref, zeta_ref, xi_ref,
                gch_ref, o_ref, state_scr, hist_scr, ext_u, ext_a, ext_b, h_scr, inv_rms_scr, q_scr, qxi_scr,
                k_scr,
                kz_scr, v_scr, s_scr, y_scr, gsilu_scr, gate_scr, qx_scr, p_scr, att_scr, sums_scr, mixed_scr,
                merged_scr, mbf_scr):
    ts = x_ref.shape[1]
    n_chunks = ts // RET_CHUNK
    n_ext = POOL_HIST + ts
    j = pl.program_id(1)

    @pl.when(j == 0)
    def _start_of_sequence():
        state_scr[...] = jnp.zeros_like(state_scr)
        hist_scr[...] = jnp.zeros_like(hist_scr)

    h_scr[...], inv_rms_scr[...] = _split_rmsnorm(x_ref[0], nmix_ref[...])

    def project(col0):
        return _dot(h_scr[...], w_in_ref[:, col0:col0 + SLAB]) * inv_rms_scr[...]

    tile_rows = pl.ds(pl.multiple_of(j * ts, ts), ts)
    cos2 = cos_ref[tile_rows, :]
    sin2 = sin_ref[tile_rows, :]

    def rope(t):
        return t * cos2 + pltpu.roll(t, RET_DIM // 2, 1) * sin2

    def store_rotated(slab, n, plain_scr, scaled_scr, row_scale_ref, mult):
        for i in range(SLAB // RET_DIM):
            hd = n * (SLAB // RET_DIM) + i
            lo = hd * RET_DIM
            t = rope(slab[:, i * RET_DIM:(i + 1) * RET_DIM])
            if mult is not None:
                t = t * mult
            plain_scr[:, lo:lo + RET_DIM] = t.astype(BF16)
            row_scale = row_scale_ref[hd]
            for c in range(n_chunks):
                rows = slice(c * RET_CHUNK, (c + 1) * RET_CHUNK)
                scaled_scr[rows, lo:lo + RET_DIM] = (t[rows] * row_scale).astype(BF16)

    for n in range(RET_WIDTH // SLAB):
        store_rotated(project(OFF_Q + n * SLAB), n, q_scr, qxi_scr, xi_ref, None)
    for n in range(RET_WIDTH // SLAB):
        store_rotated(project(OFF_K + n * SLAB), n, k_scr, kz_scr, zeta_ref, RET_DIM ** -0.5)
    for n in range(RET_WIDTH // SLAB):
        v_scr[:, n * SLAB:(n + 1) * SLAB] = project(OFF_V + n * SLAB).astype(BF16)

    def g_ret_slab(n):
        g = project(OFF_G + n * SLAB)
        gsilu_scr[:, n * SLAB:(n + 1) * SLAB] = g * jax.nn.sigmoid(g)

    def gate_slab(n):
        gate_scr[:, n * SLAB:(n + 1) * SLAB] = jax.nn.sigmoid(project(OFF_GATES + n * SLAB))

    def pool_in_slab():
        u = project(OFF_U)
        ext_u[0:POOL_HIST, :] = hist_scr[...]
        ext_u[POOL_HIST:n_ext, :] = u
        hist_scr[...] = u[ts - POOL_HIST:, :]

    def xattn_q_slab():
        qx_scr[...] = project(OFF_QX).astype(BF16)

    def retention_scores(c, hd):
        rows = slice(c * RET_CHUNK, (c + 1) * RET_CHUNK)
        cols = slice(hd * RET_DIM, (hd + 1) * RET_DIM)
        scores = lax.dot_general(q_scr[rows, cols], k_scr[rows, cols], (((1,), (1,)), ((), ())),
                                 preferred_element_type=F32)
        s_scr[rows, cols] = (scores * decay_ref[hd]).astype(BF16)

    def retention_out(c, hd):
        rows = slice(c * RET_CHUNK, (c + 1) * RET_CHUNK)
        cols = slice(hd * RET_DIM, (hd + 1) * RET_DIM)
        v = v_scr[rows, cols]
        state = state_scr[hd]
        lhs = jnp.concatenate([s_scr[rows, cols], qxi_scr[rows, cols]], axis=1)
        rhs = jnp.concatenate([v, state.astype(BF16)], axis=0)
        y_scr[rows, cols] = _dot(lhs, rhs)
        chunk_kv = lax.dot_general(kz_scr[rows, cols], v, (((0,), (0,)), ((), ())), preferred_element_type=F32)
        state_scr[hd] = state * gch_ref[hd] + chunk_kv

    def xattn_head(hd):
        cols = slice(hd * MEM_LEN, (hd + 1) * MEM_LEN)
        s = _dot(qx_scr[...], mk_ref[0, 0, :, cols])
        e = jnp.exp(s - jnp.max(s, axis=-1, keepdims=True))
        p_scr[:, cols] = (e * (1.0 / jnp.sum(e, axis=-1, keepdims=True))).astype(BF16)

    def xattn_values():
        att_scr[...] = _dot(p_scr[...], mv_ref[0, 0]).astype(BF16)

    def xattn_up():
        merged_scr[...] = gate_scr[:, 2 * D_MODEL:3 * D_MODEL] * _dot(att_scr[...], w_up_x_ref[...])

    def pool_level(k):
        src, dst = (ext_u, ext_a, ext_b, ext_a)[k], (ext_a, ext_b, ext_a, None)[k]
        lo, shift = 8 * (k + 1), 1 << k
        s = src[lo:n_ext, :] + src[lo - shift:n_ext - shift, :]
        if dst is not None:
            dst[lo:n_ext, :] = s
        lane = lax.broadcasted_iota(jnp.int32, (ts, POOL_WIDTH), 1)
        group = (lane >= k * POOL_GROUP_DIM) & (lane < (k + 1) * POOL_GROUP_DIM)
        tile_sum = s[POOL_HIST - lo:, :]
        sums_scr[...] = tile_sum if k == 0 else jnp.where(group, tile_sum, sums_scr[...])

    def pool_mix():
        lane = lax.broadcasted_iota(jnp.int32, (ts, POOL_WIDTH), 1)
        row = lax.broadcasted_iota(jnp.int32, (ts, POOL_WIDTH), 0)
        window = jnp.left_shift(2, lax.shift_right_logical(lane, 6))
        count = jnp.minimum(j * ts + row + 1, window).astype(F32)
        pooled = sums_scr[...] / count - ext_u[POOL_HIST:n_ext, :]
        mixed_scr[...] = (_dot(pooled.astype(BF16), w_pool_ref[...]) * pscale_ref[...]).astype(BF16)

    def pool_up():
        merged_scr[...] += gate_scr[:, D_MODEL:2 * D_MODEL] * _dot(mixed_scr[...], w_up_pool_ref[...])

    gates_per_branch = D_MODEL // SLAB
    part = functools.partial
    tasks = [part(g_ret_slab, n) for n in range(RET_WIDTH // SLAB)] + [pool_in_slab, xattn_q_slab]
    for hd in range(XA_HEADS):
        tasks += [part(xattn_head, hd), part(gate_slab, 2 * gates_per_branch + hd)]
    tasks += [xattn_values]
    for k in range(len(POOL_WINDOWS)):
        tasks += [part(pool_level, k), part(gate_slab, gates_per_branch + k)]
    tasks += [pool_mix, xattn_up, part(gate_slab, 0), pool_up]
    tasks += [part(gate_slab, n) for n in range(1, gates_per_branch)]

    units = [(c, hd) for c in range(n_chunks) for hd in range(RET_HEADS)]
    out_lag = 2
    assert len(tasks) >= len(units) + out_lag
    for i, task in enumerate(tasks):
        task()
        if i < len(units):
            retention_scores(*units[i])
        if out_lag <= i < len(units) + out_lag:
            retention_out(*units[i - out_lag])

    y_parts = []
    for hd in range(RET_HEADS):
        cols = slice(hd * RET_DIM, (hd + 1) * RET_DIM)
        yh = y_scr[:, cols]
        yh = yh * lax.rsqrt(jnp.mean(yh * yh, axis=-1, keepdims=True) + EPS)
        y_parts.append((yh * gsilu_scr[:, cols]).astype(BF16))
    y_ret = jnp.concatenate(y_parts, axis=1)
    for n in range(D_MODEL // SLAB):
        cols = slice(n * SLAB, (n + 1) * SLAB)
        b_ret = _dot(y_ret, w_up_ret_ref[:, cols])
        mbf_scr[:, cols] = (merged_scr[:, cols] + gate_scr[:, cols] * b_ret).astype(BF16)
    for n in range(D_MODEL // SLAB):
        cols = slice(n * SLAB, (n + 1) * SLAB)
        o_ref[0, :, cols] = x_ref[0, :, cols] + _dot(mbf_scr[...], w_out_ref[:, cols])


def _resident(shape, index_map):
    return pl.BlockSpec(shape, index_map, pipeline_mode=pl.Buffered(1))


def _mixer(x, cos2, sin2, layer, nmix, w_in, w_up_ret, w_pool_bd, pscale, w_up_pool, mk_bd, mv_bd,
           w_up_x, w_out, decay, zeta, xi, gch):
    batch, seq, _ = x.shape
    ts = SEQ_TILE
    l = layer
    whole = lambda dims: (lambda b, j: (l,) + (0,) * dims)
    const = lambda dims: (lambda b, j: (0,) * dims)
    in_specs = [
        pl.BlockSpec((1, ts, D_MODEL), lambda b, j: (b, j, 0)),
        _resident((seq, RET_DIM), const(2)),
        _resident((seq, RET_DIM), const(2)),
        _resident((None, 1, D_MODEL), whole(2)),
        _resident((None, D_MODEL, IN_WIDTH), whole(2)),
        _resident((None, RET_WIDTH, D_MODEL), whole(2)),
        _resident((None, POOL_WIDTH, POOL_WIDTH), whole(2)),
        _resident((None, 1, POOL_WIDTH), whole(2)),
        _resident((None, POOL_WIDTH, D_MODEL), whole(2)),
        pl.BlockSpec((1, 1, XA_WIDTH, XA_HEADS * MEM_LEN), lambda b, j: (l, b, 0, 0)),
        pl.BlockSpec((1, 1, XA_HEADS * MEM_LEN, XA_WIDTH), lambda b, j: (l, b, 0, 0)),
        _resident((None, XA_WIDTH, D_MODEL), whole(2)),
        _resident((None, D_MODEL, D_MODEL), whole(2)),
        _resident((RET_HEADS, RET_CHUNK, RET_CHUNK), const(3)),
        _resident((RET_HEADS, RET_CHUNK, RET_DIM), const(3)),
        _resident((RET_HEADS, RET_CHUNK, RET_DIM), const(3)),
        _resident((RET_HEADS, 1, RET_DIM), const(3)),
    ]
    scratch = [
        pltpu.VMEM((RET_HEADS, RET_DIM, RET_DIM), F32),
        pltpu.VMEM((POOL_HIST, POOL_WIDTH), F32),
        pltpu.VMEM((POOL_HIST + ts, POOL_WIDTH), F32),
        pltpu.VMEM((POOL_HIST + ts, POOL_WIDTH), F32),
        pltpu.VMEM((POOL_HIST + ts, POOL_WIDTH), F32),
        pltpu.VMEM((ts, D_MODEL), BF16),
        pltpu.VMEM((ts, SLAB), F32),
        pltpu.VMEM((ts, RET_WIDTH), BF16),
        pltpu.VMEM((ts, RET_WIDTH), BF16),
        pltpu.VMEM((ts, RET_WIDTH), BF16),
        pltpu.VMEM((ts, RET_WIDTH), BF16),
        pltpu.VMEM((ts, RET_WIDTH), BF16),
        pltpu.VMEM((ts, RET_WIDTH), BF16),
        pltpu.VMEM((ts, RET_WIDTH), F32),
        pltpu.VMEM((ts, RET_WIDTH), F32),
        pltpu.VMEM((ts, 3 * D_MODEL), F32),
        pltpu.VMEM((ts, XA_WIDTH), BF16),
        pltpu.VMEM((ts, XA_HEADS * MEM_LEN), BF16),
        pltpu.VMEM((ts, XA_WIDTH), BF16),
        pltpu.VMEM((ts, POOL_WIDTH), F32),
        pltpu.VMEM((ts, POOL_WIDTH), BF16),
        pltpu.VMEM((ts, D_MODEL), F32),
        pltpu.VMEM((ts, D_MODEL), BF16),
    ]
    return pl.pallas_call(
        _mixer_body,
        grid=(batch, seq // ts),
        in_specs=in_specs,
        out_specs=pl.BlockSpec((1, ts, D_MODEL), lambda b, j: (b, j, 0)),
        out_shape=jax.ShapeDtypeStruct(x.shape, x.dtype),
        scratch_shapes=scratch,
        compiler_params=pltpu.CompilerParams(
            dimension_semantics=("arbitrary", "arbitrary"), vmem_limit_bytes=VMEM_LIMIT_BYTES),
        name="mixer",
    )(x, cos2, sin2, nmix, w_in, w_up_ret, w_pool_bd, pscale, w_up_pool, mk_bd, mv_bd, w_up_x, w_out,
      decay, zeta, xi, gch)


def _ffn_body(x_ref, g_ref, wa_ref, wb_ref, wo_ref, fin_ref, o_ref, hid_scr, *, final_norm):
    x = x_ref[...]
    h, inv_rms = _split_rmsnorm(x, g_ref[...])
    for n in range(FFN_HIDDEN // SLAB):
        cols = slice(n * SLAB, (n + 1) * SLAB)
        a = _dot(h, wa_ref[:, cols]) * inv_rms
        b = _dot(h, wb_ref[:, cols]) * inv_rms
        hid_scr[:, cols] = (a * jax.nn.sigmoid(a) * b).astype(BF16)
    y = x + _dot(hid_scr[...], wo_ref[...])
    if final_norm:
        y = _rmsnorm_rows(y, fin_ref[...])
    o_ref[...] = y


def _ffn(x, layer, norm_ffn, w_ffn_in, w_ffn_out, final_gain, final_norm):
    tokens = x.shape[0]
    ts = FFN_TILE
    l = layer
    return pl.pallas_call(
        functools.partial(_ffn_body, final_norm=final_norm),
        grid=(tokens // ts,),
        in_specs=[
            pl.BlockSpec((ts, D_MODEL), lambda i: (i, 0)),
            _resident((None, 1, D_MODEL), lambda i: (l, 0, 0)),
            _resident((None, D_MODEL, FFN_HIDDEN), lambda i: (l, 0, 0)),
            _resident((None, D_MODEL, FFN_HIDDEN), lambda i: (l, 0, 1)),
            _resident((None, FFN_HIDDEN, D_MODEL), lambda i: (l, 0, 0)),
            _resident((1, D_MODEL), lambda i: (0, 0)),
        ],
        out_specs=pl.BlockSpec((ts, D_MODEL), lambda i: (i, 0)),
        out_shape=jax.ShapeDtypeStruct(x.shape, x.dtype),
        scratch_shapes=[pltpu.VMEM((ts, FFN_HIDDEN), BF16)],
        compiler_params=pltpu.CompilerParams(
            dimension_semantics=("arbitrary",), vmem_limit_bytes=VMEM_LIMIT_BYTES),
        name="ffn_final" if final_norm else "ffn",
    )(x, norm_ffn, w_ffn_in, w_ffn_in, w_ffn_out, final_gain)


def _retention_tables():
    heads = jnp.arange(RET_HEADS, dtype=F32)
    log_g = jnp.log(1.0 - jnp.exp2(-5.0 - heads))
    idx = jnp.arange(RET_CHUNK, dtype=F32)
    diff = idx[:, None] - idx[None, :]
    decay = jnp.where(diff[None] >= 0, jnp.exp(jnp.maximum(diff, 0.0)[None] * log_g[:, None, None]), 0.0)
    zeta = jnp.exp((RET_CHUNK - 1.0 - idx)[None, :] * log_g[:, None])
    xi = jnp.exp((idx + 1.0)[None, :] * log_g[:, None])
    gch = jnp.exp(RET_CHUNK * log_g)
    rows = lambda t: jnp.broadcast_to(t[:, :, None], (RET_HEADS, RET_CHUNK, RET_DIM))
    return decay, rows(zeta), rows(xi), jnp.broadcast_to(gch[:, None, None], (RET_HEADS, 1, RET_DIM))


def kernel(x, mem, positions, norm_mix, w_in, w_up_ret, w_pool_mix, pool_scale, w_up_pool, norm_mem,
           w_mem_kv, w_up_x, w_out, norm_ffn, w_ffn_in, w_ffn_out, final_norm):
    batch, seq, d = x.shape
    depth = w_in.shape[0]
    assert d == D_MODEL and seq % SEQ_TILE == 0 and (batch * seq) % FFN_TILE == 0
    assert mem.shape == (batch, MEM_LEN, D_MODEL) and w_in.shape[2] == IN_WIDTH

    inv_freq = ROPE_BASE ** (-jnp.arange(0, RET_DIM, 2, dtype=F32) / RET_DIM)
    ang = positions.astype(F32)[:, None] * inv_freq[None, :]
    cos2 = jnp.concatenate([jnp.cos(ang), jnp.cos(ang)], axis=-1)
    sin2 = jnp.concatenate([-jnp.sin(ang), jnp.sin(ang)], axis=-1)
    decay, zeta, xi, gch = _retention_tables()

    to_bf16 = lambda w: w.astype(BF16)
    w_in_b, w_up_ret_b, w_up_pool_b, w_up_x_b, w_out_b = map(to_bf16, (w_in, w_up_ret, w_up_pool, w_up_x, w_out))
    w_ffn_in_b, w_ffn_out_b, w_mem_kv_b = map(to_bf16, (w_ffn_in, w_ffn_out, w_mem_kv))
    eye_g = jnp.eye(len(POOL_WINDOWS), dtype=F32)
    w_pool_bd = jnp.einsum('lgcd,gh->lgchd', w_pool_mix, eye_g).reshape(depth, POOL_WIDTH, POOL_WIDTH).astype(BF16)

    mk_bd, mv_bd = _memkv(mem, norm_mem, w_mem_kv_b)

    nmix = norm_mix.reshape(depth, 1, D_MODEL)
    nffn = norm_ffn.reshape(depth, 1, D_MODEL)
    pscale = pool_scale.reshape(depth, 1, POOL_WIDTH)
    fin = final_norm.reshape(1, D_MODEL)

    for l in range(depth):
        x = _mixer(x, cos2, sin2, l, nmix, w_in_b, w_up_ret_b, w_pool_bd, pscale, w_up_pool_b, mk_bd, mv_bd,
                   w_up_x_b, w_out_b, decay, zeta, xi, gch)
        x = _ffn(x.reshape(batch * seq, d), l, nffn, w_ffn_in_b, w_ffn_out_b, fin,
                 final_norm=(l == depth - 1)).reshape(batch, seq, d)
    return x
```

```python
import functools

import jax
import jax.numpy as jnp
import numpy as np
from jax import lax
from jax.experimental import pallas as pl
from jax.experimental.pallas import tpu as pltpu

D_MODEL = 1024
MEM_LEN = 256
RET_HEADS = 4
RET_DIM = 128
RET_WIDTH = RET_HEADS * RET_DIM
RET_CHUNK = 128
ROPE_BASE = 10000.0
POOL_WINDOWS = (2, 4, 8, 16)
POOL_GROUP_DIM = 64
POOL_WIDTH = 256
POOL_HIST = 32
XA_HEADS = 4
XA_HEAD_DIM = 64
XA_WIDTH = 256
FFN_HIDDEN = 2816
EPS = 1e-6

OFF_Q = 0
OFF_K = OFF_Q + RET_WIDTH
OFF_V = OFF_K + RET_WIDTH
OFF_G = OFF_V + RET_WIDTH
OFF_U = OFF_G + RET_WIDTH
OFF_QX = OFF_U + POOL_WIDTH
OFF_GATES = OFF_QX + XA_WIDTH
IN_WIDTH = OFF_GATES + 3 * D_MODEL

SEQ_TILE = 512
FFN_TILE = 1024
SLAB = 256
VMEM_LIMIT_BYTES = 56 * 1024 * 1024

BF16 = jnp.bfloat16
F32 = jnp.float32


def _dot(a, b):
    return jnp.dot(a, b, preferred_element_type=F32)


def _rmsnorm_rows(x, g):
    return x * lax.rsqrt(jnp.mean(x * x, axis=-1, keepdims=True) + EPS) * g


def _split_rmsnorm(x):
    inv_rms = lax.rsqrt(jnp.mean(x * x, axis=-1, keepdims=True) + EPS)
    return x.astype(BF16), jnp.broadcast_to(inv_rms, (x.shape[0], SLAB))


def _memkv_body(mem_ref, g_ref, w_ref, mk_ref, mv_ref):
    m = _rmsnorm_rows(mem_ref[0], g_ref[0]).astype(BF16)
    kv = _dot(m, w_ref[0])
    keys_t = (kv[:, :XA_WIDTH] * (XA_HEAD_DIM ** -0.5)).T
    values = kv[:, XA_WIDTH:]
    head_shift = XA_HEAD_DIM.bit_length() - 1
    feature_head_k = lax.shift_right_logical(lax.broadcasted_iota(jnp.int32, (XA_WIDTH, MEM_LEN), 0), head_shift)
    feature_head_v = lax.shift_right_logical(lax.broadcasted_iota(jnp.int32, (MEM_LEN, XA_WIDTH), 1), head_shift)
    for hd in range(XA_HEADS):
        span = slice(hd * MEM_LEN, (hd + 1) * MEM_LEN)
        mk_ref[0, 0, :, span] = jnp.where(feature_head_k == hd, keys_t, 0.0).astype(BF16)
        mv_ref[0, 0, span, :] = jnp.where(feature_head_v == hd, values, 0.0).astype(BF16)


def _memkv(mem, norm_mem, w_mem_kv):
    depth = norm_mem.shape[0]
    batch = mem.shape[0]
    wide = XA_HEADS * MEM_LEN
    return pl.pallas_call(
        _memkv_body,
        grid=(depth, batch),
        in_specs=[
            pl.BlockSpec((1, MEM_LEN, D_MODEL), lambda l, b: (b, 0, 0)),
            pl.BlockSpec((1, 1, D_MODEL), lambda l, b: (l, 0, 0)),
            pl.BlockSpec((1, D_MODEL, 2 * XA_WIDTH), lambda l, b: (l, 0, 0)),
        ],
        out_specs=[pl.BlockSpec((1, 1, XA_WIDTH, wide), lambda l, b: (l, b, 0, 0)),
                   pl.BlockSpec((1, 1, wide, XA_WIDTH), lambda l, b: (l, b, 0, 0))],
        out_shape=[jax.ShapeDtypeStruct((depth, batch, XA_WIDTH, wide), BF16),
                   jax.ShapeDtypeStruct((depth, batch, wide, XA_WIDTH), BF16)],
        name="memkv",
    )(mem, norm_mem.reshape(depth, 1, D_MODEL), w_mem_kv)


def _mixer_body(x_ref, cos_ref, sin_ref, w_in_ref, w_up_ret_ref, w_pool_ref, pscale_ref, w_up_pool_ref,
                mk_ref, mv_ref, w_up_x_ref, w_out_ref, decay_ref, zeta_ref, xi_ref, gch_ref, o_ref,
                state_scr, hist_scr, ext_u, ext_a, ext_b, h_scr, inv_rms_scr, q_scr, qxi_scr, k_scr, kz_scr,
                v_scr, s_scr, y_scr, gsilu_scr, gate_scr, qx_scr, p_scr, att_scr, sums_scr, mixed_scr,
                merged_scr, mbf_scr):
    ts = x_ref.shape[1]
    n_chunks = ts // RET_CHUNK
    n_ext = POOL_HIST + ts
    j = pl.program_id(1)

    @pl.when(j == 0)
    def _start_of_sequence():
        state_scr[...] = jnp.zeros_like(state_scr)
        hist_scr[...] = jnp.zeros_like(hist_scr)

    h_scr[...], inv_rms_scr[...] = _split_rmsnorm(x_ref[0])

    def project(col0):
        return _dot(h_scr[...], w_in_ref[:, col0:col0 + SLAB]) * inv_rms_scr[...]

    tile_rows = pl.ds(pl.multiple_of(j * ts, ts), ts)
    cos2 = cos_ref[tile_rows, :]
    sin2 = sin_ref[tile_rows, :]

    def rope(t):
        return t * cos2 + pltpu.roll(t, RET_DIM // 2, 1) * sin2

    def store_rotated(slab, n, plain_scr, scaled_scr, row_scale_ref, mult):
        for i in range(SLAB // RET_DIM):
            hd = n * (SLAB // RET_DIM) + i
            lo = hd * RET_DIM
            t = rope(slab[:, i * RET_DIM:(i + 1) * RET_DIM])
            if mult is not None:
                t = t * mult
            plain_scr[:, lo:lo + RET_DIM] = t.astype(BF16)
            row_scale = row_scale_ref[hd]
            for c in range(n_chunks):
                rows = slice(c * RET_CHUNK, (c + 1) * RET_CHUNK)
                scaled_scr[rows, lo:lo + RET_DIM] = (t[rows] * row_scale).astype(BF16)

    for n in range(RET_WIDTH // SLAB):
        store_rotated(project(OFF_Q + n * SLAB), n, q_scr, qxi_scr, xi_ref, None)
    for n in range(RET_WIDTH // SLAB):
        store_rotated(project(OFF_K + n * SLAB), n, k_scr, kz_scr, zeta_ref, RET_DIM ** -0.5)
    for n in range(RET_WIDTH // SLAB):
        v_scr[:, n * SLAB:(n + 1) * SLAB] = project(OFF_V + n * SLAB).astype(BF16)

    def g_ret_slab(n):
        g = project(OFF_G + n * SLAB)
        gsilu_scr[:, n * SLAB:(n + 1) * SLAB] = g * jax.nn.sigmoid(g)

    def gate_slab(n):
        gate_scr[:, n * SLAB:(n + 1) * SLAB] = jax.nn.sigmoid(project(OFF_GATES + n * SLAB))

    def pool_in_slab():
        u = project(OFF_U)
        ext_u[0:POOL_HIST, :] = hist_scr[...]
        ext_u[POOL_HIST:n_ext, :] = u
        hist_scr[...] = u[ts - POOL_HIST:, :]

    def xattn_q_slab():
        qx_scr[...] = project(OFF_QX).astype(BF16)

    def retention_scores(c, hd):
        rows = slice(c * RET_CHUNK, (c + 1) * RET_CHUNK)
        cols = slice(hd * RET_DIM, (hd + 1) * RET_DIM)
        scores = lax.dot_general(q_scr[rows, cols], k_scr[rows, cols], (((1,), (1,)), ((), ())),
                                 preferred_element_type=F32)
        s_scr[rows, cols] = (scores * decay_ref[hd]).astype(BF16)

    def retention_out(c, hd):
        rows = slice(c * RET_CHUNK, (c + 1) * RET_CHUNK)
        cols = slice(hd * RET_DIM, (hd + 1) * RET_DIM)
        v = v_scr[rows, cols]
        state = state_scr[hd]
        lhs = jnp.concatenate([s_scr[rows, cols], qxi_scr[rows, cols]], axis=1)
        rhs = jnp.concatenate([v, state.astype(BF16)], axis=0)
        y_scr[rows, cols] = _dot(lhs, rhs)
        chunk_kv = lax.dot_general(kz_scr[rows, cols], v, (((0,), (0,)), ((), ())), preferred_element_type=F32)
        state_scr[hd] = state * gch_ref[hd] + chunk_kv

    def xattn_head(hd):
        cols = slice(hd * MEM_LEN, (hd + 1) * MEM_LEN)
        s = _dot(qx_scr[...], mk_ref[0, 0, :, cols])
        e = jnp.exp(s - jnp.max(s, axis=-1, keepdims=True))
        p_scr[:, cols] = (e * (1.0 / jnp.sum(e, axis=-1, keepdims=True))).astype(BF16)

    def xattn_values():
        att_scr[...] = _dot(p_scr[...], mv_ref[0, 0]).astype(BF16)

    def xattn_up():
        merged_scr[...] = gate_scr[:, 2 * D_MODEL:3 * D_MODEL] * _dot(att_scr[...], w_up_x_ref[...])

    def pool_level(k):
        src, dst = (ext_u, ext_a, ext_b, ext_a)[k], (ext_a, ext_b, ext_a, None)[k]
        lo, shift = 8 * (k + 1), 1 << k
        s = src[lo:n_ext, :] + src[lo - shift:n_ext - shift, :]
        if dst is not None:
            dst[lo:n_ext, :] = s
        lane = lax.broadcasted_iota(jnp.int32, (ts, POOL_WIDTH), 1)
        group = (lane >= k * POOL_GROUP_DIM) & (lane < (k + 1) * POOL_GROUP_DIM)
        tile_sum = s[POOL_HIST - lo:, :]
        sums_scr[...] = tile_sum if k == 0 else jnp.where(group, tile_sum, sums_scr[...])

    def pool_mix():
        lane = lax.broadcasted_iota(jnp.int32, (ts, POOL_WIDTH), 1)
        row = lax.broadcasted_iota(jnp.int32, (ts, POOL_WIDTH), 0)
        window = jnp.left_shift(2, lax.shift_right_logical(lane, 6))
        count = jnp.minimum(j * ts + row + 1, window).astype(F32)
        pooled = sums_scr[...] / count - ext_u[POOL_HIST:n_ext, :]
        mixed_scr[...] = (_dot(pooled.astype(BF16), w_pool_ref[...]) * pscale_ref[...]).astype(BF16)

    def pool_up():
        merged_scr[...] += gate_scr[:, D_MODEL:2 * D_MODEL] * _dot(mixed_scr[...], w_up_pool_ref[...])

    gates_per_branch = D_MODEL // SLAB
    part = functools.partial
    tasks = [part(g_ret_slab, n) for n in range(RET_WIDTH // SLAB)] + [pool_in_slab, xattn_q_slab]
    for hd in range(XA_HEADS):
        tasks += [part(xattn_head, hd), part(gate_slab, 2 * gates_per_branch + hd)]
    tasks += [xattn_values]
    for k in range(len(POOL_WINDOWS)):
        tasks += [part(pool_level, k), part(gate_slab, gates_per_branch + k)]
    tasks += [pool_mix, xattn_up, part(gate_slab, 0), pool_up]
    tasks += [part(gate_slab, n) for n in range(1, gates_per_branch)]

    units = [(c, hd) for c in range(n_chunks) for hd in range(RET_HEADS)]
    out_lag = 2
    assert len(tasks) >= len(units) + out_lag
    for i, task in enumerate(tasks):
        task()
        if i < len(units):
            retention_scores(*units[i])
        if out_lag <= i < len(units) + out_lag:
            retention_out(*units[i - out_lag])

    y_parts = []
    for hd in range(RET_HEADS):
        cols = slice(hd * RET_DIM, (hd + 1) * RET_DIM)
        yh = y_scr[:, cols]
        yh = yh * lax.rsqrt(jnp.mean(yh * yh, axis=-1, keepdims=True) + EPS)
        y_parts.append((yh * gsilu_scr[:, cols]).astype(BF16))
    y_ret = jnp.concatenate(y_parts, axis=1)
    for n in range(D_MODEL // SLAB):
        cols = slice(n * SLAB, (n + 1) * SLAB)
        b_ret = _dot(y_ret, w_up_ret_ref[:, cols])
        mbf_scr[:, cols] = (merged_scr[:, cols] + gate_scr[:, cols] * b_ret).astype(BF16)
    for n in range(D_MODEL // SLAB):
        cols = slice(n * SLAB, (n + 1) * SLAB)
        o_ref[0, :, cols] = x_ref[0, :, cols] + _dot(mbf_scr[...], w_out_ref[:, cols])


def _resident(shape, index_map):
    return pl.BlockSpec(shape, index_map, pipeline_mode=pl.Buffered(1))


def _mixer(x, cos2, sin2, layer, w_in, w_up_ret, w_pool_bd, pscale, w_up_pool, mk_bd, mv_bd,
           w_up_x, w_out, decay, zeta, xi, gch):
    batch, seq, _ = x.shape
    ts = SEQ_TILE
    l = layer
    whole = lambda dims: (lambda b, j: (l,) + (0,) * dims)
    const = lambda dims: (lambda b, j: (0,) * dims)
    in_specs = [
        pl.BlockSpec((1, ts, D_MODEL), lambda b, j: (b, j, 0)),
        _resident((seq, RET_DIM), const(2)),
        _resident((seq, RET_DIM), const(2)),
        _resident((None, D_MODEL, IN_WIDTH), whole(2)),
        _resident((None, RET_WIDTH, D_MODEL), whole(2)),
        _resident((None, POOL_WIDTH, POOL_WIDTH), whole(2)),
        _resident((None, 1, POOL_WIDTH), whole(2)),
        _resident((None, POOL_WIDTH, D_MODEL), whole(2)),
        pl.BlockSpec((1, 1, XA_WIDTH, XA_HEADS * MEM_LEN), lambda b, j: (l, b, 0, 0)),
        pl.BlockSpec((1, 1, XA_HEADS * MEM_LEN, XA_WIDTH), lambda b, j: (l, b, 0, 0)),
        _resident((None, XA_WIDTH, D_MODEL), whole(2)),
        _resident((None, D_MODEL, D_MODEL), whole(2)),
        _resident((RET_HEADS, RET_CHUNK, RET_CHUNK), const(3)),
        _resident((RET_HEADS, RET_CHUNK, RET_DIM), const(3)),
        _resident((RET_HEADS, RET_CHUNK, RET_DIM), const(3)),
        _resident((RET_HEADS, 1, RET_DIM), const(3)),
    ]
    scratch = [
        pltpu.VMEM((RET_HEADS, RET_DIM, RET_DIM), F32),
        pltpu.VMEM((POOL_HIST, POOL_WIDTH), F32),
        pltpu.VMEM((POOL_HIST + ts, POOL_WIDTH), F32),
        pltpu.VMEM((POOL_HIST + ts, POOL_WIDTH), F32),
        pltpu.VMEM((POOL_HIST + ts, POOL_WIDTH), F32),
        pltpu.VMEM((ts, D_MODEL), BF16),
        pltpu.VMEM((ts, SLAB), F32),
        pltpu.VMEM((ts, RET_WIDTH), BF16),
        pltpu.VMEM((ts, RET_WIDTH), BF16),
        pltpu.VMEM((ts, RET_WIDTH), BF16),
        pltpu.VMEM((ts, RET_WIDTH), BF16),
        pltpu.VMEM((ts, RET_WIDTH), BF16),
        pltpu.VMEM((ts, RET_WIDTH), BF16),
        pltpu.VMEM((ts, RET_WIDTH), F32),
        pltpu.VMEM((ts, RET_WIDTH), F32),
        pltpu.VMEM((ts, 3 * D_MODEL), F32),
        pltpu.VMEM((ts, XA_WIDTH), BF16),
        pltpu.VMEM((ts, XA_HEADS * MEM_LEN), BF16),
        pltpu.VMEM((ts, XA_WIDTH), BF16),
        pltpu.VMEM((ts, POOL_WIDTH), F32),
        pltpu.VMEM((ts, POOL_WIDTH), BF16),
        pltpu.VMEM((ts, D_MODEL), F32),
        pltpu.VMEM((ts, D_MODEL), BF16),
    ]
    return pl.pallas_call(
        _mixer_body,
        grid=(batch, seq // ts),
        in_specs=in_specs,
        out_specs=pl.BlockSpec((1, ts, D_MODEL), lambda b, j: (b, j, 0)),
        out_shape=jax.ShapeDtypeStruct(x.shape, x.dtype),
        scratch_shapes=scratch,
        compiler_params=pltpu.CompilerParams(
            dimension_semantics=("arbitrary", "arbitrary"), vmem_limit_bytes=VMEM_LIMIT_BYTES),
        name="mixer",
    )(x, cos2, sin2, w_in, w_up_ret, w_pool_bd, pscale, w_up_pool, mk_bd, mv_bd, w_up_x, w_out,
      decay, zeta, xi, gch)


def _ffn_body(x_ref, wa_ref, wb_ref, wo_ref, fin_ref, o_ref, hid_scr, *, final_norm):
    x = x_ref[...]
    h, inv_rms = _split_rmsnorm(x)
    for n in range(FFN_HIDDEN // SLAB):
        cols = slice(n * SLAB, (n + 1) * SLAB)
        a = _dot(h, wa_ref[:, cols]) * inv_rms
        b = _dot(h, wb_ref[:, cols]) * inv_rms
        hid_scr[:, cols] = (a * jax.nn.sigmoid(a) * b).astype(BF16)
    y = x + _dot(hid_scr[...], wo_ref[...])
    if final_norm:
        y = _rmsnorm_rows(y, fin_ref[...])
    o_ref[...] = y


def _ffn(x, layer, w_ffn_in, w_ffn_out, final_gain, final_norm):
    tokens = x.shape[0]
    ts = FFN_TILE
    l = layer
    return pl.pallas_call(
        functools.partial(_ffn_body, final_norm=final_norm),
        grid=(tokens // ts,),
        in_specs=[
            pl.BlockSpec((ts, D_MODEL), lambda i: (i, 0)),
            _resident((None, D_MODEL, FFN_HIDDEN), lambda i: (l, 0, 0)),
            _resident((None, D_MODEL, FFN_HIDDEN), lambda i: (l, 0, 1)),
            _resident((None, FFN_HIDDEN, D_MODEL), lambda i: (l, 0, 0)),
            _resident((1, D_MODEL), lambda i: (0, 0)),
        ],
        out_specs=pl.BlockSpec((ts, D_MODEL), lambda i: (i, 0)),
        out_shape=jax.ShapeDtypeStruct(x.shape, x.dtype),
        scratch_shapes=[pltpu.VMEM((ts, FFN_HIDDEN), BF16)],
        compiler_params=pltpu.CompilerParams(
            dimension_semantics=("arbitrary",), vmem_limit_bytes=VMEM_LIMIT_BYTES),
        name="ffn_final" if final_norm else "ffn",
    )(x, w_ffn_in, w_ffn_in, w_ffn_out, final_gain)


def _retention_tables():
    log_g = np.log(1.0 - np.exp2(-5.0 - np.arange(RET_HEADS, dtype=np.float64)))
    idx = np.arange(RET_CHUNK, dtype=np.float64)
    diff = idx[:, None] - idx[None, :]
    decay = np.where(diff[None] >= 0, np.exp(np.maximum(diff, 0.0)[None] * log_g[:, None, None]), 0.0)
    zeta = np.exp((RET_CHUNK - 1.0 - idx)[None, :] * log_g[:, None])
    xi = np.exp((idx + 1.0)[None, :] * log_g[:, None])
    gch = np.exp(RET_CHUNK * log_g)
    rows = lambda t: np.broadcast_to(t[:, :, None], (RET_HEADS, RET_CHUNK, RET_DIM))
    tables = (decay, rows(zeta), rows(xi), np.broadcast_to(gch[:, None, None], (RET_HEADS, 1, RET_DIM)))
    return tuple(jnp.asarray(t, dtype=F32) for t in tables)


def kernel(x, mem, positions, norm_mix, w_in, w_up_ret, w_pool_mix, pool_scale, w_up_pool, norm_mem,
           w_mem_kv, w_up_x, w_out, norm_ffn, w_ffn_in, w_ffn_out, final_norm):
    batch, seq, d = x.shape
    depth = w_in.shape[0]
    assert d == D_MODEL and seq % SEQ_TILE == 0 and (batch * seq) % FFN_TILE == 0
    assert mem.shape == (batch, MEM_LEN, D_MODEL) and w_in.shape[2] == IN_WIDTH

    inv_freq = ROPE_BASE ** (-jnp.arange(0, RET_DIM, 2, dtype=F32) / RET_DIM)
    ang = positions.astype(F32)[:, None] * inv_freq[None, :]
    cos2 = jnp.concatenate([jnp.cos(ang), jnp.cos(ang)], axis=-1)
    sin2 = jnp.concatenate([-jnp.sin(ang), jnp.sin(ang)], axis=-1)
    decay, zeta, xi, gch = _retention_tables()

    to_bf16 = lambda w: w.astype(BF16)
    w_in_b = to_bf16(norm_mix[:, :, None] * w_in)
    w_ffn_in_b = to_bf16(norm_ffn[:, :, None] * w_ffn_in)
    w_up_ret_b, w_up_pool_b, w_up_x_b, w_out_b = map(to_bf16, (w_up_ret, w_up_pool, w_up_x, w_out))
    w_ffn_out_b, w_mem_kv_b = map(to_bf16, (w_ffn_out, w_mem_kv))
    eye_g = jnp.eye(len(POOL_WINDOWS), dtype=F32)
    w_pool_bd = jnp.einsum('lgcd,gh->lgchd', w_pool_mix, eye_g).reshape(depth, POOL_WIDTH, POOL_WIDTH).astype(BF16)

    mk_bd, mv_bd = _memkv(mem, norm_mem, w_mem_kv_b)

    pscale = pool_scale.reshape(depth, 1, POOL_WIDTH)
    fin = final_norm.reshape(1, D_MODEL)

    for l in range(depth):
        x = _mixer(x, cos2, sin2, l, w_in_b, w_up_ret_b, w_pool_bd, pscale, w_up_pool_b, mk_bd, mv_bd,
                   w_up_x_b, w_out_b, decay, zeta, xi, gch)
        x = _ffn(x.reshape(batch * seq, d), l, w_ffn_in_b, w_ffn_out_b, fin,
                 final_norm=(l == depth - 1)).reshape(batch, seq, d)
    return x
```

```python
import functools

import jax
import jax.numpy as jnp
import numpy as np
from jax import lax
from jax.experimental import pallas as pl
from jax.experimental.pallas import tpu as pltpu

D_MODEL = 1024
MEM_LEN = 256
RET_HEADS = 4
RET_DIM = 128
RET_WIDTH = RET_HEADS * RET_DIM
RET_CHUNK = 128
ROPE_BASE = 10000.0
POOL_WINDOWS = (2, 4, 8, 16)
POOL_GROUP_DIM = 64
POOL_WIDTH = 256
POOL_HIST = 32
XA_HEADS = 4
XA_HEAD_DIM = 64
XA_WIDTH = 256
FFN_HIDDEN = 2816
EPS = 1e-6

OFF_Q = 0
OFF_K = OFF_Q + RET_WIDTH
OFF_V = OFF_K + RET_WIDTH
OFF_G = OFF_V + RET_WIDTH
OFF_U = OFF_G + RET_WIDTH
OFF_QX = OFF_U + POOL_WIDTH
OFF_GATES = OFF_QX + XA_WIDTH
IN_WIDTH = OFF_GATES + 3 * D_MODEL

SEQ_TILE = 512
SUBTILES = 2
FFN_TILE = 1024
SLAB = 256
VMEM_LIMIT_BYTES = 56 * 1024 * 1024

BF16 = jnp.bfloat16
F32 = jnp.float32


def _dot(a, b):
    return jnp.dot(a, b, preferred_element_type=F32)


def _rmsnorm_rows(x, g):
    return x * lax.rsqrt(jnp.mean(x * x, axis=-1, keepdims=True) + EPS) * g


def _split_rmsnorm(x):
    inv_rms = lax.rsqrt(jnp.mean(x * x, axis=-1, keepdims=True) + EPS)
    return x.astype(BF16), jnp.broadcast_to(inv_rms, (x.shape[0], SLAB))


def _memkv_body(mem_ref, g_ref, w_ref, mk_ref, mv_ref):
    m = _rmsnorm_rows(mem_ref[0], g_ref[0]).astype(BF16)
    kv = _dot(m, w_ref[0])
    keys_t = (kv[:, :XA_WIDTH] * (XA_HEAD_DIM ** -0.5)).T
    values = kv[:, XA_WIDTH:]
    head_shift = XA_HEAD_DIM.bit_length() - 1
    feature_head_k = lax.shift_right_logical(lax.broadcasted_iota(jnp.int32, (XA_WIDTH, MEM_LEN), 0), head_shift)
    feature_head_v = lax.shift_right_logical(lax.broadcasted_iota(jnp.int32, (MEM_LEN, XA_WIDTH), 1), head_shift)
    for hd in range(XA_HEADS):
        span = slice(hd * MEM_LEN, (hd + 1) * MEM_LEN)
        mk_ref[0, 0, :, span] = jnp.where(feature_head_k == hd, keys_t, 0.0).astype(BF16)
        mv_ref[0, 0, span, :] = jnp.where(feature_head_v == hd, values, 0.0).astype(BF16)


def _memkv(mem, norm_mem, w_mem_kv):
    depth = norm_mem.shape[0]
    batch = mem.shape[0]
    wide = XA_HEADS * MEM_LEN
    return pl.pallas_call(
        _memkv_body,
        grid=(depth, batch),
        in_specs=[
            pl.BlockSpec((1, MEM_LEN, D_MODEL), lambda l, b: (b, 0, 0)),
            pl.BlockSpec((1, 1, D_MODEL), lambda l, b: (l, 0, 0)),
            pl.BlockSpec((1, D_MODEL, 2 * XA_WIDTH), lambda l, b: (l, 0, 0)),
        ],
        out_specs=[pl.BlockSpec((1, 1, XA_WIDTH, wide), lambda l, b: (l, b, 0, 0)),
                   pl.BlockSpec((1, 1, wide, XA_WIDTH), lambda l, b: (l, b, 0, 0))],
        out_shape=[jax.ShapeDtypeStruct((depth, batch, XA_WIDTH, wide), BF16),
                   jax.ShapeDtypeStruct((depth, batch, wide, XA_WIDTH), BF16)],
        name="memkv",
    )(mem, norm_mem.reshape(depth, 1, D_MODEL), w_mem_kv)


def _mixer_tile(sub, x_ref, cos_ref, sin_ref, w_in_ref, w_up_ret_ref, w_pool_ref, pscale_ref, w_up_pool_ref,
                mk_ref, mv_ref, w_up_x_ref, w_out_ref, decay_ref, zeta_ref, xi_ref, gch_ref, o_ref,
                state_scr, hist_scr, ext_u, ext_a, ext_b, h_scr, inv_rms_scr, q_scr, qxi_scr, k_scr, kz_scr,
                v_scr, s_scr, y_scr, gsilu_scr, gate_scr, qx_scr, p_scr, att_scr, sums_scr, mixed_scr,
                merged_scr, mbf_scr):
    ts = SEQ_TILE
    n_chunks = ts // RET_CHUNK
    n_ext = POOL_HIST + ts
    j = pl.program_id(1) * SUBTILES + sub
    rows = slice(sub * ts, (sub + 1) * ts)

    if sub == 0:
        @pl.when(j == 0)
        def _start_of_sequence():
            state_scr[...] = jnp.zeros_like(state_scr)
            hist_scr[...] = jnp.zeros_like(hist_scr)

    h_scr[...], inv_rms_scr[...] = _split_rmsnorm(x_ref[0, rows, :])

    def project(col0):
        return _dot(h_scr[...], w_in_ref[:, col0:col0 + SLAB]) * inv_rms_scr[...]

    tile_rows = pl.ds(pl.multiple_of(j * ts, ts), ts)
    cos2 = cos_ref[tile_rows, :]
    sin2 = sin_ref[tile_rows, :]

    def rope(t):
        return t * cos2 + pltpu.roll(t, RET_DIM // 2, 1) * sin2

    def store_rotated(slab, n, plain_scr, scaled_scr, row_scale_ref, mult):
        for i in range(SLAB // RET_DIM):
            hd = n * (SLAB // RET_DIM) + i
            lo = hd * RET_DIM
            t = rope(slab[:, i * RET_DIM:(i + 1) * RET_DIM])
            if mult is not None:
                t = t * mult
            plain_scr[:, lo:lo + RET_DIM] = t.astype(BF16)
            row_scale = row_scale_ref[hd]
            for c in range(n_chunks):
                rows = slice(c * RET_CHUNK, (c + 1) * RET_CHUNK)
                scaled_scr[rows, lo:lo + RET_DIM] = (t[rows] * row_scale).astype(BF16)

    for n in range(RET_WIDTH // SLAB):
        store_rotated(project(OFF_Q + n * SLAB), n, q_scr, qxi_scr, xi_ref, None)
    for n in range(RET_WIDTH // SLAB):
        store_rotated(project(OFF_K + n * SLAB), n, k_scr, kz_scr, zeta_ref, RET_DIM ** -0.5)
    for n in range(RET_WIDTH // SLAB):
        v_scr[:, n * SLAB:(n + 1) * SLAB] = project(OFF_V + n * SLAB).astype(BF16)

    def g_ret_slab(n):
        g = project(OFF_G + n * SLAB)
        gsilu_scr[:, n * SLAB:(n + 1) * SLAB] = g * jax.nn.sigmoid(g)

    def gate_slab(n):
        gate_scr[:, n * SLAB:(n + 1) * SLAB] = jax.nn.sigmoid(project(OFF_GATES + n * SLAB))

    def pool_in_slab():
        u = project(OFF_U)
        ext_u[0:POOL_HIST, :] = hist_scr[...]
        ext_u[POOL_HIST:n_ext, :] = u
        hist_scr[...] = u[ts - POOL_HIST:, :]

    def xattn_q_slab():
        qx_scr[...] = project(OFF_QX).astype(BF16)

    def retention_scores(c, hd):
        rows = slice(c * RET_CHUNK, (c + 1) * RET_CHUNK)
        cols = slice(hd * RET_DIM, (hd + 1) * RET_DIM)
        scores = lax.dot_general(q_scr[rows, cols], k_scr[rows, cols], (((1,), (1,)), ((), ())),
                                 preferred_element_type=F32)
        s_scr[rows, cols] = (scores * decay_ref[hd]).astype(BF16)

    def retention_out(c, hd):
        rows = slice(c * RET_CHUNK, (c + 1) * RET_CHUNK)
        cols = slice(hd * RET_DIM, (hd + 1) * RET_DIM)
        v = v_scr[rows, cols]
        state = state_scr[hd]
        lhs = jnp.concatenate([s_scr[rows, cols], qxi_scr[rows, cols]], axis=1)
        rhs = jnp.concatenate([v, state.astype(BF16)], axis=0)
        y_scr[rows, cols] = _dot(lhs, rhs)
        chunk_kv = lax.dot_general(kz_scr[rows, cols], v, (((0,), (0,)), ((), ())), preferred_element_type=F32)
        state_scr[hd] = state * gch_ref[hd] + chunk_kv

    def xattn_head(hd):
        cols = slice(hd * MEM_LEN, (hd + 1) * MEM_LEN)
        s = _dot(qx_scr[...], mk_ref[0, 0, :, cols])
        e = jnp.exp(s - jnp.max(s, axis=-1, keepdims=True))
        p_scr[:, cols] = (e * (1.0 / jnp.sum(e, axis=-1, keepdims=True))).astype(BF16)

    def xattn_values():
        att_scr[...] = _dot(p_scr[...], mv_ref[0, 0]).astype(BF16)

    def xattn_up():
        merged_scr[...] = gate_scr[:, 2 * D_MODEL:3 * D_MODEL] * _dot(att_scr[...], w_up_x_ref[...])

    def pool_level(k):
        src, dst = (ext_u, ext_a, ext_b, ext_a)[k], (ext_a, ext_b, ext_a, None)[k]
        lo, shift = 8 * (k + 1), 1 << k
        s = src[lo:n_ext, :] + src[lo - shift:n_ext - shift, :]
        if dst is not None:
            dst[lo:n_ext, :] = s
        lane = lax.broadcasted_iota(jnp.int32, (ts, POOL_WIDTH), 1)
        group = (lane >= k * POOL_GROUP_DIM) & (lane < (k + 1) * POOL_GROUP_DIM)
        tile_sum = s[POOL_HIST - lo:, :]
        sums_scr[...] = tile_sum if k == 0 else jnp.where(group, tile_sum, sums_scr[...])

    def pool_mix():
        lane = lax.broadcasted_iota(jnp.int32, (ts, POOL_WIDTH), 1)
        row = lax.broadcasted_iota(jnp.int32, (ts, POOL_WIDTH), 0)
        window = jnp.left_shift(2, lax.shift_right_logical(lane, 6))
        count = jnp.minimum(j * ts + row + 1, window).astype(F32)
        pooled = sums_scr[...] / count - ext_u[POOL_HIST:n_ext, :]
        mixed_scr[...] = (_dot(pooled.astype(BF16), w_pool_ref[...]) * pscale_ref[...]).astype(BF16)

    def pool_up():
        merged_scr[...] += gate_scr[:, D_MODEL:2 * D_MODEL] * _dot(mixed_scr[...], w_up_pool_ref[...])

    gates_per_branch = D_MODEL // SLAB
    part = functools.partial
    tasks = [part(g_ret_slab, n) for n in range(RET_WIDTH // SLAB)] + [pool_in_slab, xattn_q_slab]
    for hd in range(XA_HEADS):
        tasks += [part(xattn_head, hd), part(gate_slab, 2 * gates_per_branch + hd)]
    tasks += [xattn_values]
    for k in range(len(POOL_WINDOWS)):
        tasks += [part(pool_level, k), part(gate_slab, gates_per_branch + k)]
    tasks += [pool_mix, xattn_up, part(gate_slab, 0), pool_up]
    tasks += [part(gate_slab, n) for n in range(1, gates_per_branch)]

    units = [(c, hd) for c in range(n_chunks) for hd in range(RET_HEADS)]
    out_lag = 2
    assert len(tasks) >= len(units) + out_lag
    for i, task in enumerate(tasks):
        task()
        if i < len(units):
            retention_scores(*units[i])
        if out_lag <= i < len(units) + out_lag:
            retention_out(*units[i - out_lag])

    y_parts = []
    for hd in range(RET_HEADS):
        cols = slice(hd * RET_DIM, (hd + 1) * RET_DIM)
        yh = y_scr[:, cols]
        yh = yh * lax.rsqrt(jnp.mean(yh * yh, axis=-1, keepdims=True) + EPS)
        y_parts.append((yh * gsilu_scr[:, cols]).astype(BF16))
    y_ret = jnp.concatenate(y_parts, axis=1)
    for n in range(D_MODEL // SLAB):
        cols = slice(n * SLAB, (n + 1) * SLAB)
        b_ret = _dot(y_ret, w_up_ret_ref[:, cols])
        mbf_scr[:, cols] = (merged_scr[:, cols] + gate_scr[:, cols] * b_ret).astype(BF16)
    for n in range(D_MODEL // SLAB):
        cols = slice(n * SLAB, (n + 1) * SLAB)
        o_ref[0, rows, cols] = x_ref[0, rows, cols] + _dot(mbf_scr[...], w_out_ref[:, cols])


def _mixer_body(*refs):
    for sub in range(SUBTILES):
        _mixer_tile(sub, *refs)


def _resident(shape, index_map):
    return pl.BlockSpec(shape, index_map, pipeline_mode=pl.Buffered(1))


def _mixer(x, cos2, sin2, layer, w_in, w_up_ret, w_pool_bd, pscale, w_up_pool, mk_bd, mv_bd,
           w_up_x, w_out, decay, zeta, xi, gch):
    batch, seq, _ = x.shape
    ts = SEQ_TILE
    l = layer
    whole = lambda dims: (lambda b, j: (l,) + (0,) * dims)
    const = lambda dims: (lambda b, j: (0,) * dims)
    in_specs = [
        pl.BlockSpec((1, SUBTILES * ts, D_MODEL), lambda b, j: (b, j, 0)),
        _resident((seq, RET_DIM), const(2)),
        _resident((seq, RET_DIM), const(2)),
        _resident((None, D_MODEL, IN_WIDTH), whole(2)),
        _resident((None, RET_WIDTH, D_MODEL), whole(2)),
        _resident((None, POOL_WIDTH, POOL_WIDTH), whole(2)),
        _resident((None, 1, POOL_WIDTH), whole(2)),
        _resident((None, POOL_WIDTH, D_MODEL), whole(2)),
        pl.BlockSpec((1, 1, XA_WIDTH, XA_HEADS * MEM_LEN), lambda b, j: (l, b, 0, 0)),
        pl.BlockSpec((1, 1, XA_HEADS * MEM_LEN, XA_WIDTH), lambda b, j: (l, b, 0, 0)),
        _resident((None, XA_WIDTH, D_MODEL), whole(2)),
        _resident((None, D_MODEL, D_MODEL), whole(2)),
        _resident((RET_HEADS, RET_CHUNK, RET_CHUNK), const(3)),
        _resident((RET_HEADS, RET_CHUNK, RET_DIM), const(3)),
        _resident((RET_HEADS, RET_CHUNK, RET_DIM), const(3)),
        _resident((RET_HEADS, 1, RET_DIM), const(3)),
    ]
    scratch = [
        pltpu.VMEM((RET_HEADS, RET_DIM, RET_DIM), F32),
        pltpu.VMEM((POOL_HIST, POOL_WIDTH), F32),
        pltpu.VMEM((POOL_HIST + ts, POOL_WIDTH), F32),
        pltpu.VMEM((POOL_HIST + ts, POOL_WIDTH), F32),
        pltpu.VMEM((POOL_HIST + ts, POOL_WIDTH), F32),
        pltpu.VMEM((ts, D_MODEL), BF16),
        pltpu.VMEM((ts, SLAB), F32),
        pltpu.VMEM((ts, RET_WIDTH), BF16),
        pltpu.VMEM((ts, RET_WIDTH), BF16),
        pltpu.VMEM((ts, RET_WIDTH), BF16),
        pltpu.VMEM((ts, RET_WIDTH), BF16),
        pltpu.VMEM((ts, RET_WIDTH), BF16),
        pltpu.VMEM((ts, RET_WIDTH), BF16),
        pltpu.VMEM((ts, RET_WIDTH), F32),
        pltpu.VMEM((ts, RET_WIDTH), F32),
        pltpu.VMEM((ts, 3 * D_MODEL), F32),
        pltpu.VMEM((ts, XA_WIDTH), BF16),
        pltpu.VMEM((ts, XA_HEADS * MEM_LEN), BF16),
        pltpu.VMEM((ts, XA_WIDTH), BF16),
        pltpu.VMEM((ts, POOL_WIDTH), F32),
        pltpu.VMEM((ts, POOL_WIDTH), BF16),
        pltpu.VMEM((ts, D_MODEL), F32),
        pltpu.VMEM((ts, D_MODEL), BF16),
    ]
    return pl.pallas_call(
        _mixer_body,
        grid=(batch, seq // (SUBTILES * ts)),
        in_specs=in_specs,
        out_specs=pl.BlockSpec((1, SUBTILES * ts, D_MODEL), lambda b, j: (b, j, 0)),
        out_shape=jax.ShapeDtypeStruct(x.shape, x.dtype),
        scratch_shapes=scratch,
        compiler_params=pltpu.CompilerParams(
            dimension_semantics=("arbitrary", "arbitrary"), vmem_limit_bytes=VMEM_LIMIT_BYTES),
        name="mixer",
    )(x, cos2, sin2, w_in, w_up_ret, w_pool_bd, pscale, w_up_pool, mk_bd, mv_bd, w_up_x, w_out,
      decay, zeta, xi, gch)


def _ffn_body(x_ref, wa_ref, wb_ref, wo_ref, fin_ref, o_ref, hid_scr, *, final_norm):
    x = x_ref[...]
    h, inv_rms = _split_rmsnorm(x)
    for n in range(FFN_HIDDEN // SLAB):
        cols = slice(n * SLAB, (n + 1) * SLAB)
        a = _dot(h, wa_ref[:, cols]) * inv_rms
        b = _dot(h, wb_ref[:, cols]) * inv_rms
        hid_scr[:, cols] = (a * jax.nn.sigmoid(a) * b).astype(BF16)
    y = x + _dot(hid_scr[...], wo_ref[...])
    if final_norm:
        y = _rmsnorm_rows(y, fin_ref[...])
    o_ref[...] = y


def _ffn(x, layer, w_ffn_in, w_ffn_out, final_gain, final_norm):
    tokens = x.shape[0]
    ts = FFN_TILE
    l = layer
    return pl.pallas_call(
        functools.partial(_ffn_body, final_norm=final_norm),
        grid=(tokens // ts,),
        in_specs=[
            pl.BlockSpec((ts, D_MODEL), lambda i: (i, 0)),
            _resident((None, D_MODEL, FFN_HIDDEN), lambda i: (l, 0, 0)),
            _resident((None, D_MODEL, FFN_HIDDEN), lambda i: (l, 0, 1)),
            _resident((None, FFN_HIDDEN, D_MODEL), lambda i: (l, 0, 0)),
            _resident((1, D_MODEL), lambda i: (0, 0)),
        ],
        out_specs=pl.BlockSpec((ts, D_MODEL), lambda i: (i, 0)),
        out_shape=jax.ShapeDtypeStruct(x.shape, x.dtype),
        scratch_shapes=[pltpu.VMEM((ts, FFN_HIDDEN), BF16)],
        compiler_params=pltpu.CompilerParams(
            dimension_semantics=("arbitrary",), vmem_limit_bytes=VMEM_LIMIT_BYTES),
        name="ffn_final" if final_norm else "ffn",
    )(x, w_ffn_in, w_ffn_in, w_ffn_out, final_gain)


def _retention_tables():
    log_g = np.log(1.0 - np.exp2(-5.0 - np.arange(RET_HEADS, dtype=np.float64)))
    idx = np.arange(RET_CHUNK, dtype=np.float64)
    diff = idx[:, None] - idx[None, :]
    decay = np.where(diff[None] >= 0, np.exp(np.maximum(diff, 0.0)[None] * log_g[:, None, None]), 0.0)
    zeta = np.exp((RET_CHUNK - 1.0 - idx)[None, :] * log_g[:, None])
    xi = np.exp((idx + 1.0)[None, :] * log_g[:, None])
    gch = np.exp(RET_CHUNK * log_g)
    rows = lambda t: np.broadcast_to(t[:, :, None], (RET_HEADS, RET_CHUNK, RET_DIM))
    tables = (decay, rows(zeta), rows(xi), np.broadcast_to(gch[:, None, None], (RET_HEADS, 1, RET_DIM)))
    return tuple(jnp.asarray(t, dtype=F32) for t in tables)


def kernel(x, mem, positions, norm_mix, w_in, w_up_ret, w_pool_mix, pool_scale, w_up_pool, norm_mem,
           w_mem_kv, w_up_x, w_out, norm_ffn, w_ffn_in, w_ffn_out, final_norm):
    batch, seq, d = x.shape
    depth = w_in.shape[0]
    assert d == D_MODEL and seq % (SEQ_TILE * SUBTILES) == 0 and (batch * seq) % FFN_TILE == 0
    assert mem.shape == (batch, MEM_LEN, D_MODEL) and w_in.shape[2] == IN_WIDTH

    inv_freq = ROPE_BASE ** (-jnp.arange(0, RET_DIM, 2, dtype=F32) / RET_DIM)
    ang = positions.astype(F32)[:, None] * inv_freq[None, :]
    cos2 = jnp.concatenate([jnp.cos(ang), jnp.cos(ang)], axis=-1)
    sin2 = jnp.concatenate([-jnp.sin(ang), jnp.sin(ang)], axis=-1)
    decay, zeta, xi, gch = _retention_tables()

    to_bf16 = lambda w: w.astype(BF16)
    w_in_b = to_bf16(norm_mix[:, :, None] * w_in)
    w_ffn_in_b = to_bf16(norm_ffn[:, :, None] * w_ffn_in)
    w_up_ret_b, w_up_pool_b, w_up_x_b, w_out_b = map(to_bf16, (w_up_ret, w_up_pool, w_up_x, w_out))
    w_ffn_out_b, w_mem_kv_b = map(to_bf16, (w_ffn_out, w_mem_kv))
    eye_g = jnp.eye(len(POOL_WINDOWS), dtype=F32)
    w_pool_bd = jnp.einsum('lgcd,gh->lgchd', w_pool_mix, eye_g).reshape(depth, POOL_WIDTH, POOL_WIDTH).astype(BF16)

    mk_bd, mv_bd = _memkv(mem, norm_mem, w_mem_kv_b)

    pscale = pool_scale.reshape(depth, 1, POOL_WIDTH)
    fin = final_norm.reshape(1, D_MODEL)

    for l in range(depth):
        x = _mixer(x, cos2, sin2, l, w_in_b, w_up_ret_b, w_pool_bd, pscale, w_up_pool_b, mk_bd, mv_bd,
                   w_up_x_b, w_out_b, decay, zeta, xi, gch)
        x = _ffn(x.reshape(batch * seq, d), l, w_ffn_in_b, w_ffn_out_b, fin,
                 final_norm=(l == depth - 1)).reshape(batch, seq, d)
    return x
```

```python
import functools

import jax
import jax.numpy as jnp
import numpy as np
from jax import lax
from jax.experimental import pallas as pl
from jax.experimental.pallas import tpu as pltpu

D_MODEL = 1024
MEM_LEN = 256
RET_HEADS = 4
RET_DIM = 128
RET_WIDTH = RET_HEADS * RET_DIM
RET_CHUNK = 128
ROPE_BASE = 10000.0
POOL_WINDOWS = (2, 4, 8, 16)
POOL_GROUP_DIM = 64
POOL_WIDTH = 256
POOL_HIST = 32
XA_HEADS = 4
XA_HEAD_DIM = 64
XA_WIDTH = 256
FFN_HIDDEN = 2816
EPS = 1e-6

OFF_Q = 0
OFF_K = OFF_Q + RET_WIDTH
OFF_V = OFF_K + RET_WIDTH
OFF_G = OFF_V + RET_WIDTH
OFF_U = OFF_G + RET_WIDTH
OFF_QX = OFF_U + POOL_WIDTH
OFF_GATES = OFF_QX + XA_WIDTH
IN_WIDTH = OFF_GATES + 3 * D_MODEL

SEQ_TILE = 512
SUBTILES = 2
FFN_TILE = 1024
SLAB = 256
VMEM_LIMIT_BYTES = 56 * 1024 * 1024

BF16 = jnp.bfloat16
F32 = jnp.float32


def _dot(a, b):
    return jnp.dot(a, b, preferred_element_type=F32)


def _rmsnorm_rows(x, g):
    return x * lax.rsqrt(jnp.mean(x * x, axis=-1, keepdims=True) + EPS) * g


def _split_rmsnorm(x):
    inv_rms = lax.rsqrt(jnp.mean(x * x, axis=-1, keepdims=True) + EPS)
    return x.astype(BF16), jnp.broadcast_to(inv_rms, (x.shape[0], SLAB))


def _memkv_body(mem_ref, g_ref, w_ref, mk_ref, mv_ref):
    m = _rmsnorm_rows(mem_ref[0], g_ref[0]).astype(BF16)
    kv = _dot(m, w_ref[0])
    keys_t = (kv[:, :XA_WIDTH] * (XA_HEAD_DIM ** -0.5)).T
    values = kv[:, XA_WIDTH:]
    head_shift = XA_HEAD_DIM.bit_length() - 1
    feature_head_k = lax.shift_right_logical(lax.broadcasted_iota(jnp.int32, (XA_WIDTH, MEM_LEN), 0), head_shift)
    feature_head_v = lax.shift_right_logical(lax.broadcasted_iota(jnp.int32, (MEM_LEN, XA_WIDTH), 1), head_shift)
    for hd in range(XA_HEADS):
        span = slice(hd * MEM_LEN, (hd + 1) * MEM_LEN)
        mk_ref[0, 0, :, span] = jnp.where(feature_head_k == hd, keys_t, 0.0).astype(BF16)
        mv_ref[0, 0, span, :] = jnp.where(feature_head_v == hd, values, 0.0).astype(BF16)


def _memkv(mem, norm_mem, w_mem_kv):
    depth = norm_mem.shape[0]
    batch = mem.shape[0]
    wide = XA_HEADS * MEM_LEN
    return pl.pallas_call(
        _memkv_body,
        grid=(depth, batch),
        in_specs=[
            pl.BlockSpec((1, MEM_LEN, D_MODEL), lambda l, b: (b, 0, 0)),
            pl.BlockSpec((1, 1, D_MODEL), lambda l, b: (l, 0, 0)),
            pl.BlockSpec((1, D_MODEL, 2 * XA_WIDTH), lambda l, b: (l, 0, 0)),
        ],
        out_specs=[pl.BlockSpec((1, 1, XA_WIDTH, wide), lambda l, b: (l, b, 0, 0)),
                   pl.BlockSpec((1, 1, wide, XA_WIDTH), lambda l, b: (l, b, 0, 0))],
        out_shape=[jax.ShapeDtypeStruct((depth, batch, XA_WIDTH, wide), BF16),
                   jax.ShapeDtypeStruct((depth, batch, wide, XA_WIDTH), BF16)],
        name="memkv",
    )(mem, norm_mem.reshape(depth, 1, D_MODEL), w_mem_kv)


def _mixer_tile(sub, x_ref, cos_ref, sin_ref, w_in_ref, w_up_ret_ref, w_pool_ref, pscale_ref, w_up_pool_ref,
                mk_ref, mv_ref, w_up_x_ref, w_out_ref, decay_ref, zeta_ref, xi_ref, gch_ref, o_ref,
                state_scr, hist_scr, ext_u, ext_a, ext_b, h_scr, inv_rms_scr, q_scr, qxi_scr, k_scr, kz_scr,
                v_scr, s_scr, y_scr, gsilu_scr, gate_scr, qx_scr, p_scr, att_scr, sums_scr, mixed_scr,
                merged_scr, mbf_scr):
    ts = SEQ_TILE
    n_chunks = ts // RET_CHUNK
    n_ext = POOL_HIST + ts
    j = pl.program_id(1) * SUBTILES + sub
    rows = pl.ds(pl.multiple_of(sub * ts, ts), ts)

    @pl.when(j == 0)
    def _start_of_sequence():
        state_scr[...] = jnp.zeros_like(state_scr)
        hist_scr[...] = jnp.zeros_like(hist_scr)

    h_scr[...], inv_rms_scr[...] = _split_rmsnorm(x_ref[0, rows, :])

    def project(col0):
        return _dot(h_scr[...], w_in_ref[:, col0:col0 + SLAB]) * inv_rms_scr[...]

    tile_rows = pl.ds(pl.multiple_of(j * ts, ts), ts)
    cos2 = cos_ref[tile_rows, :]
    sin2 = sin_ref[tile_rows, :]

    def rope(t):
        return t * cos2 + pltpu.roll(t, RET_DIM // 2, 1) * sin2

    def store_rotated(slab, n, plain_scr, scaled_scr, row_scale_ref, mult):
        for i in range(SLAB // RET_DIM):
            hd = n * (SLAB // RET_DIM) + i
            lo = hd * RET_DIM
            t = rope(slab[:, i * RET_DIM:(i + 1) * RET_DIM])
            if mult is not None:
                t = t * mult
            plain_scr[:, lo:lo + RET_DIM] = t.astype(BF16)
            row_scale = row_scale_ref[hd]
            for c in range(n_chunks):
                rows = slice(c * RET_CHUNK, (c + 1) * RET_CHUNK)
                scaled_scr[rows, lo:lo + RET_DIM] = (t[rows] * row_scale).astype(BF16)

    for n in range(RET_WIDTH // SLAB):
        store_rotated(project(OFF_Q + n * SLAB), n, q_scr, qxi_scr, xi_ref, None)
    for n in range(RET_WIDTH // SLAB):
        store_rotated(project(OFF_K + n * SLAB), n, k_scr, kz_scr, zeta_ref, RET_DIM ** -0.5)
    for n in range(RET_WIDTH // SLAB):
        v_scr[:, n * SLAB:(n + 1) * SLAB] = project(OFF_V + n * SLAB).astype(BF16)

    def g_ret_slab(n):
        g = project(OFF_G + n * SLAB)
        gsilu_scr[:, n * SLAB:(n + 1) * SLAB] = g * jax.nn.sigmoid(g)

    def gate_slab(n):
        gate_scr[:, n * SLAB:(n + 1) * SLAB] = jax.nn.sigmoid(project(OFF_GATES + n * SLAB))

    def pool_in_slab():
        u = project(OFF_U)
        ext_u[0:POOL_HIST, :] = hist_scr[...]
        ext_u[POOL_HIST:n_ext, :] = u
        hist_scr[...] = u[ts - POOL_HIST:, :]

    def xattn_q_slab():
        qx_scr[...] = project(OFF_QX).astype(BF16)

    def retention_scores(c, hd):
        rows = slice(c * RET_CHUNK, (c + 1) * RET_CHUNK)
        cols = slice(hd * RET_DIM, (hd + 1) * RET_DIM)
        scores = lax.dot_general(q_scr[rows, cols], k_scr[rows, cols], (((1,), (1,)), ((), ())),
                                 preferred_element_type=F32)
        s_scr[rows, cols] = (scores * decay_ref[hd]).astype(BF16)

    def retention_out(c, hd):
        rows = slice(c * RET_CHUNK, (c + 1) * RET_CHUNK)
        cols = slice(hd * RET_DIM, (hd + 1) * RET_DIM)
        v = v_scr[rows, cols]
        state = state_scr[hd]
        lhs = jnp.concatenate([s_scr[rows, cols], qxi_scr[rows, cols]], axis=1)
        rhs = jnp.concatenate([v, state.astype(BF16)], axis=0)
        y_scr[rows, cols] = _dot(lhs, rhs)
        chunk_kv = lax.dot_general(kz_scr[rows, cols], v, (((0,), (0,)), ((), ())), preferred_element_type=F32)
        state_scr[hd] = state * gch_ref[hd] + chunk_kv

    def xattn_head(hd):
        cols = slice(hd * MEM_LEN, (hd + 1) * MEM_LEN)
        s = _dot(qx_scr[...], mk_ref[0, 0, :, cols])
        e = jnp.exp(s - jnp.max(s, axis=-1, keepdims=True))
        p_scr[:, cols] = (e * (1.0 / jnp.sum(e, axis=-1, keepdims=True))).astype(BF16)

    def xattn_values():
        att_scr[...] = _dot(p_scr[...], mv_ref[0, 0]).astype(BF16)

    def xattn_up():
        merged_scr[...] = gate_scr[:, 2 * D_MODEL:3 * D_MODEL] * _dot(att_scr[...], w_up_x_ref[...])

    def pool_level(k):
        src, dst = (ext_u, ext_a, ext_b, ext_a)[k], (ext_a, ext_b, ext_a, None)[k]
        lo, shift = 8 * (k + 1), 1 << k
        s = src[lo:n_ext, :] + src[lo - shift:n_ext - shift, :]
        if dst is not None:
            dst[lo:n_ext, :] = s
        lane = lax.broadcasted_iota(jnp.int32, (ts, POOL_WIDTH), 1)
        group = (lane >= k * POOL_GROUP_DIM) & (lane < (k + 1) * POOL_GROUP_DIM)
        tile_sum = s[POOL_HIST - lo:, :]
        sums_scr[...] = tile_sum if k == 0 else jnp.where(group, tile_sum, sums_scr[...])

    def pool_mix():
        lane = lax.broadcasted_iota(jnp.int32, (ts, POOL_WIDTH), 1)
        row = lax.broadcasted_iota(jnp.int32, (ts, POOL_WIDTH), 0)
        window = jnp.left_shift(2, lax.shift_right_logical(lane, 6))
        count = jnp.minimum(j * ts + row + 1, window).astype(F32)
        pooled = sums_scr[...] / count - ext_u[POOL_HIST:n_ext, :]
        mixed_scr[...] = (_dot(pooled.astype(BF16), w_pool_ref[...]) * pscale_ref[...]).astype(BF16)

    def pool_up():
        merged_scr[...] += gate_scr[:, D_MODEL:2 * D_MODEL] * _dot(mixed_scr[...], w_up_pool_ref[...])

    gates_per_branch = D_MODEL // SLAB
    part = functools.partial
    tasks = [part(g_ret_slab, n) for n in range(RET_WIDTH // SLAB)] + [pool_in_slab, xattn_q_slab]
    for hd in range(XA_HEADS):
        tasks += [part(xattn_head, hd), part(gate_slab, 2 * gates_per_branch + hd)]
    tasks += [xattn_values]
    for k in range(len(POOL_WINDOWS)):
        tasks += [part(pool_level, k), part(gate_slab, gates_per_branch + k)]
    tasks += [pool_mix, xattn_up, part(gate_slab, 0), pool_up]
    tasks += [part(gate_slab, n) for n in range(1, gates_per_branch)]

    units = [(c, hd) for c in range(n_chunks) for hd in range(RET_HEADS)]
    out_lag = 2
    assert len(tasks) >= len(units) + out_lag
    for i, task in enumerate(tasks):
        task()
        if i < len(units):
            retention_scores(*units[i])
        if out_lag <= i < len(units) + out_lag:
            retention_out(*units[i - out_lag])

    y_parts = []
    for hd in range(RET_HEADS):
        cols = slice(hd * RET_DIM, (hd + 1) * RET_DIM)
        yh = y_scr[:, cols]
        yh = yh * lax.rsqrt(jnp.mean(yh * yh, axis=-1, keepdims=True) + EPS)
        y_parts.append((yh * gsilu_scr[:, cols]).astype(BF16))
    y_ret = jnp.concatenate(y_parts, axis=1)
    for n in range(D_MODEL // SLAB):
        cols = slice(n * SLAB, (n + 1) * SLAB)
        b_ret = _dot(y_ret, w_up_ret_ref[:, cols])
        mbf_scr[:, cols] = (merged_scr[:, cols] + gate_scr[:, cols] * b_ret).astype(BF16)
    for n in range(D_MODEL // SLAB):
        cols = slice(n * SLAB, (n + 1) * SLAB)
        o_ref[0, rows, cols] = x_ref[0, rows, cols] + _dot(mbf_scr[...], w_out_ref[:, cols])


def _mixer_body(*refs):
    def one_tile(sub, carry):
        _mixer_tile(sub, *refs)
        return carry

    lax.fori_loop(0, SUBTILES, one_tile, 0)


def _resident(shape, index_map):
    return pl.BlockSpec(shape, index_map, pipeline_mode=pl.Buffered(1))


def _mixer(x, cos2, sin2, layer, w_in, w_up_ret, w_pool_bd, pscale, w_up_pool, mk_bd, mv_bd,
           w_up_x, w_out, decay, zeta, xi, gch):
    batch, seq, _ = x.shape
    ts = SEQ_TILE
    l = layer
    whole = lambda dims: (lambda b, j: (l,) + (0,) * dims)
    const = lambda dims: (lambda b, j: (0,) * dims)
    in_specs = [
        pl.BlockSpec((1, SUBTILES * ts, D_MODEL), lambda b, j: (b, j, 0)),
        _resident((seq, RET_DIM), const(2)),
        _resident((seq, RET_DIM), const(2)),
        _resident((None, D_MODEL, IN_WIDTH), whole(2)),
        _resident((None, RET_WIDTH, D_MODEL), whole(2)),
        _resident((None, POOL_WIDTH, POOL_WIDTH), whole(2)),
        _resident((None, 1, POOL_WIDTH), whole(2)),
        _resident((None, POOL_WIDTH, D_MODEL), whole(2)),
        pl.BlockSpec((1, 1, XA_WIDTH, XA_HEADS * MEM_LEN), lambda b, j: (l, b, 0, 0)),
        pl.BlockSpec((1, 1, XA_HEADS * MEM_LEN, XA_WIDTH), lambda b, j: (l, b, 0, 0)),
        _resident((None, XA_WIDTH, D_MODEL), whole(2)),
        _resident((None, D_MODEL, D_MODEL), whole(2)),
        _resident((RET_HEADS, RET_CHUNK, RET_CHUNK), const(3)),
        _resident((RET_HEADS, RET_CHUNK, RET_DIM), const(3)),
        _resident((RET_HEADS, RET_CHUNK, RET_DIM), const(3)),
        _resident((RET_HEADS, 1, RET_DIM), const(3)),
    ]
    scratch = [
        pltpu.VMEM((RET_HEADS, RET_DIM, RET_DIM), F32),
        pltpu.VMEM((POOL_HIST, POOL_WIDTH), F32),
        pltpu.VMEM((POOL_HIST + ts, POOL_WIDTH), F32),
        pltpu.VMEM((POOL_HIST + ts, POOL_WIDTH), F32),
        pltpu.VMEM((POOL_HIST + ts, POOL_WIDTH), F32),
        pltpu.VMEM((ts, D_MODEL), BF16),
        pltpu.VMEM((ts, SLAB), F32),
        pltpu.VMEM((ts, RET_WIDTH), BF16),
        pltpu.VMEM((ts, RET_WIDTH), BF16),
        pltpu.VMEM((ts, RET_WIDTH), BF16),
        pltpu.VMEM((ts, RET_WIDTH), BF16),
        pltpu.VMEM((ts, RET_WIDTH), BF16),
        pltpu.VMEM((ts, RET_WIDTH), BF16),
        pltpu.VMEM((ts, RET_WIDTH), F32),
        pltpu.VMEM((ts, RET_WIDTH), F32),
        pltpu.VMEM((ts, 3 * D_MODEL), F32),
        pltpu.VMEM((ts, XA_WIDTH), BF16),
        pltpu.VMEM((ts, XA_HEADS * MEM_LEN), BF16),
        pltpu.VMEM((ts, XA_WIDTH), BF16),
        pltpu.VMEM((ts, POOL_WIDTH), F32),
        pltpu.VMEM((ts, POOL_WIDTH), BF16),
        pltpu.VMEM((ts, D_MODEL), F32),
        pltpu.VMEM((ts, D_MODEL), BF16),
    ]
    return pl.pallas_call(
        _mixer_body,
        grid=(batch, seq // (SUBTILES * ts)),
        in_specs=in_specs,
        out_specs=pl.BlockSpec((1, SUBTILES * ts, D_MODEL), lambda b, j: (b, j, 0)),
        out_shape=jax.ShapeDtypeStruct(x.shape, x.dtype),
        scratch_shapes=scratch,
        compiler_params=pltpu.CompilerParams(
            dimension_semantics=("arbitrary", "arbitrary"), vmem_limit_bytes=VMEM_LIMIT_BYTES),
        name="mixer",
    )(x, cos2, sin2, w_in, w_up_ret, w_pool_bd, pscale, w_up_pool, mk_bd, mv_bd, w_up_x, w_out,
      decay, zeta, xi, gch)


def _ffn_body(x_ref, wa_ref, wb_ref, wo_ref, fin_ref, o_ref, hid_scr, *, final_norm):
    x = x_ref[...]
    h, inv_rms = _split_rmsnorm(x)
    for n in range(FFN_HIDDEN // SLAB):
        cols = slice(n * SLAB, (n + 1) * SLAB)
        a = _dot(h, wa_ref[:, cols]) * inv_rms
        b = _dot(h, wb_ref[:, cols]) * inv_rms
        hid_scr[:, cols] = (a * jax.nn.sigmoid(a) * b).astype(BF16)
    y = x + _dot(hid_scr[...], wo_ref[...])
    if final_norm:
        y = _rmsnorm_rows(y, fin_ref[...])
    o_ref[...] = y


def _ffn(x, layer, w_ffn_in, w_ffn_out, final_gain, final_norm):
    tokens = x.shape[0]
    ts = FFN_TILE
    l = layer
    return pl.pallas_call(
        functools.partial(_ffn_body, final_norm=final_norm),
        grid=(tokens // ts,),
        in_specs=[
            pl.BlockSpec((ts, D_MODEL), lambda i: (i, 0)),
            _resident((None, D_MODEL, FFN_HIDDEN), lambda i: (l, 0, 0)),
            _resident((None, D_MODEL, FFN_HIDDEN), lambda i: (l, 0, 1)),
            _resident((None, FFN_HIDDEN, D_MODEL), lambda i: (l, 0, 0)),
            _resident((1, D_MODEL), lambda i: (0, 0)),
        ],
        out_specs=pl.BlockSpec((ts, D_MODEL), lambda i: (i, 0)),
        out_shape=jax.ShapeDtypeStruct(x.shape, x.dtype),
        scratch_shapes=[pltpu.VMEM((ts, FFN_HIDDEN), BF16)],
        compiler_params=pltpu.CompilerParams(
            dimension_semantics=("arbitrary",), vmem_limit_bytes=VMEM_LIMIT_BYTES),
        name="ffn_final" if final_norm else "ffn",
    )(x, w_ffn_in, w_ffn_in, w_ffn_out, final_gain)


def _retention_tables():
    log_g = np.log(1.0 - np.exp2(-5.0 - np.arange(RET_HEADS, dtype=np.float64)))
    idx = np.arange(RET_CHUNK, dtype=np.float64)
    diff = idx[:, None] - idx[None, :]
    decay = np.where(diff[None] >= 0, np.exp(np.maximum(diff, 0.0)[None] * log_g[:, None, None]), 0.0)
    zeta = np.exp((RET_CHUNK - 1.0 - idx)[None, :] * log_g[:, None])
    xi = np.exp((idx + 1.0)[None, :] * log_g[:, None])
    gch = np.exp(RET_CHUNK * log_g)
    rows = lambda t: np.broadcast_to(t[:, :, None], (RET_HEADS, RET_CHUNK, RET_DIM))
    tables = (decay, rows(zeta), rows(xi), np.broadcast_to(gch[:, None, None], (RET_HEADS, 1, RET_DIM)))
    return tuple(jnp.asarray(t, dtype=F32) for t in tables)


def kernel(x, mem, positions, norm_mix, w_in, w_up_ret, w_pool_mix, pool_scale, w_up_pool, norm_mem,
           w_mem_kv, w_up_x, w_out, norm_ffn, w_ffn_in, w_ffn_out, final_norm):
    batch, seq, d = x.shape
    depth = w_in.shape[0]
    assert d == D_MODEL and seq % (SEQ_TILE * SUBTILES) == 0 and (batch * seq) % FFN_TILE == 0
    assert mem.shape == (batch, MEM_LEN, D_MODEL) and w_in.shape[2] == IN_WIDTH

    inv_freq = ROPE_BASE ** (-jnp.arange(0, RET_DIM, 2, dtype=F32) / RET_DIM)
    ang = positions.astype(F32)[:, None] * inv_freq[None, :]
    cos2 = jnp.concatenate([jnp.cos(ang), jnp.cos(ang)], axis=-1)
    sin2 = jnp.concatenate([-jnp.sin(ang), jnp.sin(ang)], axis=-1)
    decay, zeta, xi, gch = _retention_tables()

    to_bf16 = lambda w: w.astype(BF16)
    w_in_b = to_bf16(norm_mix[:, :, None] * w_in)
    w_ffn_in_b = to_bf16(norm_ffn[:, :, None] * w_ffn_in)
    w_up_ret_b, w_up_pool_b, w_up_x_b, w_out_b = map(to_bf16, (w_up_ret, w_up_pool, w_up_x, w_out))
    w_ffn_out_b, w_mem_kv_b = map(to_bf16, (w_ffn_out, w_mem_kv))
    eye_g = jnp.eye(len(POOL_WINDOWS), dtype=F32)
    w_pool_bd = jnp.einsum('lgcd,gh->lgchd', w_pool_mix, eye_g).reshape(depth, POOL_WIDTH, POOL_WIDTH).astype(BF16)

    mk_bd, mv_bd = _memkv(mem, norm_mem, w_mem_kv_b)

    pscale = pool_scale.reshape(depth, 1, POOL_WIDTH)
    fin = final_norm.reshape(1, D_MODEL)

    for l in range(depth):
        x = _mixer(x, cos2, sin2, l, w_in_b, w_up_ret_b, w_pool_bd, pscale, w_up_pool_b, mk_bd, mv_bd,
                   w_up_x_b, w_out_b, decay, zeta, xi, gch)
        x = _ffn(x.reshape(batch * seq, d), l, w_ffn_in_b, w_ffn_out_b, fin,
                 final_norm=(l == depth - 1)).reshape(batch, seq, d)
    return x
```

```python
import functools

import jax
import jax.numpy as jnp
import numpy as np
from jax import lax
from jax.experimental import pallas as pl
from jax.experimental.pallas import tpu as pltpu

D_MODEL = 1024
MEM_LEN = 256
RET_HEADS = 4
RET_DIM = 128
RET_WIDTH = RET_HEADS * RET_DIM
RET_CHUNK = 128
ROPE_BASE = 10000.0
POOL_WINDOWS = (2, 4, 8, 16)
POOL_GROUP_DIM = 64
POOL_WIDTH = 256
POOL_HIST = 32
XA_HEADS = 4
XA_HEAD_DIM = 64
XA_WIDTH = 256
FFN_HIDDEN = 2816
EPS = 1e-6

OFF_Q = 0
OFF_K = OFF_Q + RET_WIDTH
OFF_V = OFF_K + RET_WIDTH
OFF_G = OFF_V + RET_WIDTH
OFF_U = OFF_G + RET_WIDTH
OFF_QX = OFF_U + POOL_WIDTH
OFF_GATES = OFF_QX + XA_WIDTH
IN_WIDTH = OFF_GATES + 3 * D_MODEL

SEQ_TILE = 512
SUBTILES = 2
FFN_TILE = 1024
SLAB = 256
VMEM_LIMIT_BYTES = 56 * 1024 * 1024

BF16 = jnp.bfloat16
F32 = jnp.float32


def _dot(a, b):
    return jnp.dot(a, b, preferred_element_type=F32)


def _rmsnorm_rows(x, g):
    return x * lax.rsqrt(jnp.mean(x * x, axis=-1, keepdims=True) + EPS) * g


def _split_rmsnorm(x):
    inv_rms = lax.rsqrt(jnp.mean(x * x, axis=-1, keepdims=True) + EPS)
    return x.astype(BF16), jnp.broadcast_to(inv_rms, (x.shape[0], SLAB))


def _memkv_body(mem_ref, g_ref, w_ref, mk_ref, mv_ref):
    batch = mem_ref.shape[0]
    mem = mem_ref[...].reshape(batch * MEM_LEN, D_MODEL)
    m = _rmsnorm_rows(mem, g_ref[0]).astype(BF16)
    kv = _dot(m, w_ref[0])
    head_shift = XA_HEAD_DIM.bit_length() - 1
    feature_head_k = lax.shift_right_logical(lax.broadcasted_iota(jnp.int32, (XA_WIDTH, MEM_LEN), 0), head_shift)
    feature_head_v = lax.shift_right_logical(lax.broadcasted_iota(jnp.int32, (MEM_LEN, XA_WIDTH), 1), head_shift)
    for b in range(batch):
        tokens = slice(b * MEM_LEN, (b + 1) * MEM_LEN)
        keys_t = (kv[tokens, :XA_WIDTH] * (XA_HEAD_DIM ** -0.5)).T
        values = kv[tokens, XA_WIDTH:]
        for hd in range(XA_HEADS):
            span = slice(hd * MEM_LEN, (hd + 1) * MEM_LEN)
            mk_ref[0, b, :, span] = jnp.where(feature_head_k == hd, keys_t, 0.0).astype(BF16)
            mv_ref[0, b, span, :] = jnp.where(feature_head_v == hd, values, 0.0).astype(BF16)


def _memkv(mem, norm_mem, w_mem_kv):
    depth = norm_mem.shape[0]
    batch = mem.shape[0]
    wide = XA_HEADS * MEM_LEN
    return pl.pallas_call(
        _memkv_body,
        grid=(depth,),
        in_specs=[
            pl.BlockSpec((batch, MEM_LEN, D_MODEL), lambda l: (0, 0, 0)),
            pl.BlockSpec((1, 1, D_MODEL), lambda l: (l, 0, 0)),
            pl.BlockSpec((1, D_MODEL, 2 * XA_WIDTH), lambda l: (l, 0, 0)),
        ],
        out_specs=[pl.BlockSpec((1, batch, XA_WIDTH, wide), lambda l: (l, 0, 0, 0)),
                   pl.BlockSpec((1, batch, wide, XA_WIDTH), lambda l: (l, 0, 0, 0))],
        out_shape=[jax.ShapeDtypeStruct((depth, batch, XA_WIDTH, wide), BF16),
                   jax.ShapeDtypeStruct((depth, batch, wide, XA_WIDTH), BF16)],
        name="memkv",
    )(mem, norm_mem.reshape(depth, 1, D_MODEL), w_mem_kv)


def _mixer_tile(sub, x_ref, cos_ref, sin_ref, w_in_ref, w_up_ret_ref, w_pool_ref, pscale_ref, w_up_pool_ref,
                mk_ref, mv_ref, w_up_x_ref, w_out_ref, decay_ref, zeta_ref, xi_ref, gch_ref, o_ref,
                state_scr, hist_scr, ext_u, ext_a, ext_b, h_scr, inv_rms_scr, q_scr, qxi_scr, k_scr, kz_scr,
                v_scr, s_scr, y_scr, gsilu_scr, gate_scr, qx_scr, p_scr, att_scr, sums_scr, mixed_scr,
                merged_scr, mbf_scr):
    ts = SEQ_TILE
    n_chunks = ts // RET_CHUNK
    n_ext = POOL_HIST + ts
    j = pl.program_id(1) * SUBTILES + sub
    rows = pl.ds(pl.multiple_of(sub * ts, ts), ts)

    @pl.when(j == 0)
    def _start_of_sequence():
        state_scr[...] = jnp.zeros_like(state_scr)
        hist_scr[...] = jnp.zeros_like(hist_scr)

    h_scr[...], inv_rms_scr[...] = _split_rmsnorm(x_ref[0, rows, :])

    def project(col0):
        return _dot(h_scr[...], w_in_ref[:, col0:col0 + SLAB]) * inv_rms_scr[...]

    tile_rows = pl.ds(pl.multiple_of(j * ts, ts), ts)
    cos2 = cos_ref[tile_rows, :]
    sin2 = sin_ref[tile_rows, :]

    def rope(t):
        return t * cos2 + pltpu.roll(t, RET_DIM // 2, 1) * sin2

    def store_rotated(slab, n, plain_scr, scaled_scr, row_scale_ref, mult):
        for i in range(SLAB // RET_DIM):
            hd = n * (SLAB // RET_DIM) + i
            lo = hd * RET_DIM
            t = rope(slab[:, i * RET_DIM:(i + 1) * RET_DIM])
            if mult is not None:
                t = t * mult
            plain_scr[:, lo:lo + RET_DIM] = t.astype(BF16)
            row_scale = row_scale_ref[hd]
            for c in range(n_chunks):
                rows = slice(c * RET_CHUNK, (c + 1) * RET_CHUNK)
                scaled_scr[rows, lo:lo + RET_DIM] = (t[rows] * row_scale).astype(BF16)

    for n in range(RET_WIDTH // SLAB):
        store_rotated(project(OFF_Q + n * SLAB), n, q_scr, qxi_scr, xi_ref, None)
    for n in range(RET_WIDTH // SLAB):
        store_rotated(project(OFF_K + n * SLAB), n, k_scr, kz_scr, zeta_ref, RET_DIM ** -0.5)
    for n in range(RET_WIDTH // SLAB):
        v_scr[:, n * SLAB:(n + 1) * SLAB] = project(OFF_V + n * SLAB).astype(BF16)

    def g_ret_slab(n):
        g = project(OFF_G + n * SLAB)
        gsilu_scr[:, n * SLAB:(n + 1) * SLAB] = g * jax.nn.sigmoid(g)

    def gate_slab(n):
        gate_scr[:, n * SLAB:(n + 1) * SLAB] = jax.nn.sigmoid(project(OFF_GATES + n * SLAB))

    def pool_in_slab():
        u = project(OFF_U)
        ext_u[0:POOL_HIST, :] = hist_scr[...]
        ext_u[POOL_HIST:n_ext, :] = u
        hist_scr[...] = u[ts - POOL_HIST:, :]

    def xattn_q_slab():
        qx_scr[...] = project(OFF_QX).astype(BF16)

    def retention_scores(c, hd):
        rows = slice(c * RET_CHUNK, (c + 1) * RET_CHUNK)
        cols = slice(hd * RET_DIM, (hd + 1) * RET_DIM)
        scores = lax.dot_general(q_scr[rows, cols], k_scr[rows, cols], (((1,), (1,)), ((), ())),
                                 preferred_element_type=F32)
        s_scr[rows, cols] = (scores * decay_ref[hd]).astype(BF16)

    def retention_out(c, hd):
        rows = slice(c * RET_CHUNK, (c + 1) * RET_CHUNK)
        cols = slice(hd * RET_DIM, (hd + 1) * RET_DIM)
        v = v_scr[rows, cols]
        state = state_scr[hd]
        lhs = jnp.concatenate([s_scr[rows, cols], qxi_scr[rows, cols]], axis=1)
        rhs = jnp.concatenate([v, state.astype(BF16)], axis=0)
        y_scr[rows, cols] = _dot(lhs, rhs)
        chunk_kv = lax.dot_general(kz_scr[rows, cols], v, (((0,), (0,)), ((), ())), preferred_element_type=F32)
        state_scr[hd] = state * gch_ref[hd] + chunk_kv

    def xattn_head(hd):
        cols = slice(hd * MEM_LEN, (hd + 1) * MEM_LEN)
        s = _dot(qx_scr[...], mk_ref[0, 0, :, cols])
        e = jnp.exp(s - jnp.max(s, axis=-1, keepdims=True))
        p_scr[:, cols] = (e * (1.0 / jnp.sum(e, axis=-1, keepdims=True))).astype(BF16)

    def xattn_values():
        att_scr[...] = _dot(p_scr[...], mv_ref[0, 0]).astype(BF16)

    def xattn_up():
        merged_scr[...] = gate_scr[:, 2 * D_MODEL:3 * D_MODEL] * _dot(att_scr[...], w_up_x_ref[...])

    def pool_level(k):
        src, dst = (ext_u, ext_a, ext_b, ext_a)[k], (ext_a, ext_b, ext_a, None)[k]
        lo, shift = 8 * (k + 1), 1 << k
        s = src[lo:n_ext, :] + src[lo - shift:n_ext - shift, :]
        if dst is not None:
            dst[lo:n_ext, :] = s
        lane = lax.broadcasted_iota(jnp.int32, (ts, POOL_WIDTH), 1)
        group = (lane >= k * POOL_GROUP_DIM) & (lane < (k + 1) * POOL_GROUP_DIM)
        tile_sum = s[POOL_HIST - lo:, :]
        sums_scr[...] = tile_sum if k == 0 else jnp.where(group, tile_sum, sums_scr[...])

    def pool_mix():
        lane = lax.broadcasted_iota(jnp.int32, (ts, POOL_WIDTH), 1)
        row = lax.broadcasted_iota(jnp.int32, (ts, POOL_WIDTH), 0)
        window = jnp.left_shift(2, lax.shift_right_logical(lane, 6))
        count = jnp.minimum(j * ts + row + 1, window).astype(F32)
        pooled = sums_scr[...] / count - ext_u[POOL_HIST:n_ext, :]
        mixed_scr[...] = (_dot(pooled.astype(BF16), w_pool_ref[...]) * pscale_ref[...]).astype(BF16)

    def pool_up():
        merged_scr[...] += gate_scr[:, D_MODEL:2 * D_MODEL] * _dot(mixed_scr[...], w_up_pool_ref[...])

    gates_per_branch = D_MODEL // SLAB
    part = functools.partial
    tasks = [part(g_ret_slab, n) for n in range(RET_WIDTH // SLAB)] + [pool_in_slab, xattn_q_slab]
    for hd in range(XA_HEADS):
        tasks += [part(xattn_head, hd), part(gate_slab, 2 * gates_per_branch + hd)]
    tasks += [xattn_values]
    for k in range(len(POOL_WINDOWS)):
        tasks += [part(pool_level, k), part(gate_slab, gates_per_branch + k)]
    tasks += [pool_mix, xattn_up, part(gate_slab, 0), pool_up]
    tasks += [part(gate_slab, n) for n in range(1, gates_per_branch)]

    units = [(c, hd) for c in range(n_chunks) for hd in range(RET_HEADS)]
    out_lag = 2
    assert len(tasks) >= len(units) + out_lag
    for i, task in enumerate(tasks):
        task()
        if i < len(units):
            retention_scores(*units[i])
        if out_lag <= i < len(units) + out_lag:
            retention_out(*units[i - out_lag])

    y_parts = []
    for hd in range(RET_HEADS):
        cols = slice(hd * RET_DIM, (hd + 1) * RET_DIM)
        yh = y_scr[:, cols]
        yh = yh * lax.rsqrt(jnp.mean(yh * yh, axis=-1, keepdims=True) + EPS)
        y_parts.append((yh * gsilu_scr[:, cols]).astype(BF16))
    y_ret = jnp.concatenate(y_parts, axis=1)
    for n in range(D_MODEL // SLAB):
        cols = slice(n * SLAB, (n + 1) * SLAB)
        b_ret = _dot(y_ret, w_up_ret_ref[:, cols])
        mbf_scr[:, cols] = (merged_scr[:, cols] + gate_scr[:, cols] * b_ret).astype(BF16)
    for n in range(D_MODEL // SLAB):
        cols = slice(n * SLAB, (n + 1) * SLAB)
        o_ref[0, rows, cols] = x_ref[0, rows, cols] + _dot(mbf_scr[...], w_out_ref[:, cols])


def _mixer_body(*refs):
    def one_tile(sub, carry):
        _mixer_tile(sub, *refs)
        return carry

    lax.fori_loop(0, SUBTILES, one_tile, 0)


def _resident(shape, index_map):
    return pl.BlockSpec(shape, index_map, pipeline_mode=pl.Buffered(1))


def _mixer(x, cos2, sin2, layer, w_in, w_up_ret, w_pool_bd, pscale, w_up_pool, mk_bd, mv_bd,
           w_up_x, w_out, decay, zeta, xi, gch):
    batch, seq, _ = x.shape
    ts = SEQ_TILE
    l = layer
    whole = lambda dims: (lambda b, j: (l,) + (0,) * dims)
    const = lambda dims: (lambda b, j: (0,) * dims)
    in_specs = [
        pl.BlockSpec((1, SUBTILES * ts, D_MODEL), lambda b, j: (b, j, 0)),
        _resident((seq, RET_DIM), const(2)),
        _resident((seq, RET_DIM), const(2)),
        _resident((None, D_MODEL, IN_WIDTH), whole(2)),
        _resident((None, RET_WIDTH, D_MODEL), whole(2)),
        _resident((None, POOL_WIDTH, POOL_WIDTH), whole(2)),
        _resident((None, 1, POOL_WIDTH), whole(2)),
        _resident((None, POOL_WIDTH, D_MODEL), whole(2)),
        pl.BlockSpec((1, 1, XA_WIDTH, XA_HEADS * MEM_LEN), lambda b, j: (l, b, 0, 0)),
        pl.BlockSpec((1, 1, XA_HEADS * MEM_LEN, XA_WIDTH), lambda b, j: (l, b, 0, 0)),
        _resident((None, XA_WIDTH, D_MODEL), whole(2)),
        _resident((None, D_MODEL, D_MODEL), whole(2)),
        _resident((RET_HEADS, RET_CHUNK, RET_CHUNK), const(3)),
        _resident((RET_HEADS, RET_CHUNK, RET_DIM), const(3)),
        _resident((RET_HEADS, RET_CHUNK, RET_DIM), const(3)),
        _resident((RET_HEADS, 1, RET_DIM), const(3)),
    ]
    scratch = [
        pltpu.VMEM((RET_HEADS, RET_DIM, RET_DIM), F32),
        pltpu.VMEM((POOL_HIST, POOL_WIDTH), F32),
        pltpu.VMEM((POOL_HIST + ts, POOL_WIDTH), F32),
        pltpu.VMEM((POOL_HIST + ts, POOL_WIDTH), F32),
        pltpu.VMEM((POOL_HIST + ts, POOL_WIDTH), F32),
        pltpu.VMEM((ts, D_MODEL), BF16),
        pltpu.VMEM((ts, SLAB), F32),
        pltpu.VMEM((ts, RET_WIDTH), BF16),
        pltpu.VMEM((ts, RET_WIDTH), BF16),
        pltpu.VMEM((ts, RET_WIDTH), BF16),
        pltpu.VMEM((ts, RET_WIDTH), BF16),
        pltpu.VMEM((ts, RET_WIDTH), BF16),
        pltpu.VMEM((ts, RET_WIDTH), BF16),
        pltpu.VMEM((ts, RET_WIDTH), F32),
        pltpu.VMEM((ts, RET_WIDTH), F32),
        pltpu.VMEM((ts, 3 * D_MODEL), F32),
        pltpu.VMEM((ts, XA_WIDTH), BF16),
        pltpu.VMEM((ts, XA_HEADS * MEM_LEN), BF16),
        pltpu.VMEM((ts, XA_WIDTH), BF16),
        pltpu.VMEM((ts, POOL_WIDTH), F32),
        pltpu.VMEM((ts, POOL_WIDTH), BF16),
        pltpu.VMEM((ts, D_MODEL), F32),
        pltpu.VMEM((ts, D_MODEL), BF16),
    ]
    return pl.pallas_call(
        _mixer_body,
        grid=(batch, seq // (SUBTILES * ts)),
        in_specs=in_specs,
        out_specs=pl.BlockSpec((1, SUBTILES * ts, D_MODEL), lambda b, j: (b, j, 0)),
        out_shape=jax.ShapeDtypeStruct(x.shape, x.dtype),
        scratch_shapes=scratch,
        compiler_params=pltpu.CompilerParams(
            dimension_semantics=("arbitrary", "arbitrary"), vmem_limit_bytes=VMEM_LIMIT_BYTES),
        name="mixer",
    )(x, cos2, sin2, w_in, w_up_ret, w_pool_bd, pscale, w_up_pool, mk_bd, mv_bd, w_up_x, w_out,
      decay, zeta, xi, gch)


def _ffn_body(x_ref, wa_ref, wb_ref, wo_ref, fin_ref, o_ref, hid_scr, *, final_norm):
    x = x_ref[...]
    h, inv_rms = _split_rmsnorm(x)
    for n in range(FFN_HIDDEN // SLAB):
        cols = slice(n * SLAB, (n + 1) * SLAB)
        a = _dot(h, wa_ref[:, cols]) * inv_rms
        b = _dot(h, wb_ref[:, cols]) * inv_rms
        hid_scr[:, cols] = (a * jax.nn.sigmoid(a) * b).astype(BF16)
    y = x + _dot(hid_scr[...], wo_ref[...])
    if final_norm:
        y = _rmsnorm_rows(y, fin_ref[...])
    o_ref[...] = y


def _ffn(x, layer, w_ffn_in, w_ffn_out, final_gain, final_norm):
    tokens = x.shape[0]
    ts = FFN_TILE
    l = layer
    return pl.pallas_call(
        functools.partial(_ffn_body, final_norm=final_norm),
        grid=(tokens // ts,),
        in_specs=[
            pl.BlockSpec((ts, D_MODEL), lambda i: (i, 0)),
            _resident((None, D_MODEL, FFN_HIDDEN), lambda i: (l, 0, 0)),
            _resident((None, D_MODEL, FFN_HIDDEN), lambda i: (l, 0, 1)),
            _resident((None, FFN_HIDDEN, D_MODEL), lambda i: (l, 0, 0)),
            _resident((1, D_MODEL), lambda i: (0, 0)),
        ],
        out_specs=pl.BlockSpec((ts, D_MODEL), lambda i: (i, 0)),
        out_shape=jax.ShapeDtypeStruct(x.shape, x.dtype),
        scratch_shapes=[pltpu.VMEM((ts, FFN_HIDDEN), BF16)],
        compiler_params=pltpu.CompilerParams(
            dimension_semantics=("arbitrary",), vmem_limit_bytes=VMEM_LIMIT_BYTES),
        name="ffn_final" if final_norm else "ffn",
    )(x, w_ffn_in, w_ffn_in, w_ffn_out, final_gain)


def _retention_tables():
    log_g = np.log(1.0 - np.exp2(-5.0 - np.arange(RET_HEADS, dtype=np.float64)))
    idx = np.arange(RET_CHUNK, dtype=np.float64)
    diff = idx[:, None] - idx[None, :]
    decay = np.where(diff[None] >= 0, np.exp(np.maximum(diff, 0.0)[None] * log_g[:, None, None]), 0.0)
    zeta = np.exp((RET_CHUNK - 1.0 - idx)[None, :] * log_g[:, None])
    xi = np.exp((idx + 1.0)[None, :] * log_g[:, None])
    gch = np.exp(RET_CHUNK * log_g)
    rows = lambda t: np.broadcast_to(t[:, :, None], (RET_HEADS, RET_CHUNK, RET_DIM))
    tables = (decay, rows(zeta), rows(xi), np.broadcast_to(gch[:, None, None], (RET_HEADS, 1, RET_DIM)))
    return tuple(jnp.asarray(t, dtype=F32) for t in tables)


def kernel(x, mem, positions, norm_mix, w_in, w_up_ret, w_pool_mix, pool_scale, w_up_pool, norm_mem,
           w_mem_kv, w_up_x, w_out, norm_ffn, w_ffn_in, w_ffn_out, final_norm):
    batch, seq, d = x.shape
    depth = w_in.shape[0]
    assert d == D_MODEL and seq % (SEQ_TILE * SUBTILES) == 0 and (batch * seq) % FFN_TILE == 0
    assert mem.shape == (batch, MEM_LEN, D_MODEL) and w_in.shape[2] == IN_WIDTH

    inv_freq = ROPE_BASE ** (-jnp.arange(0, RET_DIM, 2, dtype=F32) / RET_DIM)
    ang = positions.astype(F32)[:, None] * inv_freq[None, :]
    cos2 = jnp.concatenate([jnp.cos(ang), jnp.cos(ang)], axis=-1)
    sin2 = jnp.concatenate([-jnp.sin(ang), jnp.sin(ang)], axis=-1)
    decay, zeta, xi, gch = _retention_tables()

    to_bf16 = lambda w: w.astype(BF16)
    w_in_b = to_bf16(norm_mix[:, :, None] * w_in)
    w_ffn_in_b = to_bf16(norm_ffn[:, :, None] * w_ffn_in)
    w_up_ret_b, w_up_pool_b, w_up_x_b, w_out_b = map(to_bf16, (w_up_ret, w_up_pool, w_up_x, w_out))
    w_ffn_out_b, w_mem_kv_b = map(to_bf16, (w_ffn_out, w_mem_kv))
    eye_g = jnp.eye(len(POOL_WINDOWS), dtype=F32)
    w_pool_bd = jnp.einsum('lgcd,gh->lgchd', w_pool_mix, eye_g).reshape(depth, POOL_WIDTH, POOL_WIDTH).astype(BF16)

    mk_bd, mv_bd = _memkv(mem, norm_mem, w_mem_kv_b)

    pscale = pool_scale.reshape(depth, 1, POOL_WIDTH)
    fin = final_norm.reshape(1, D_MODEL)

    for l in range(depth):
        x = _mixer(x, cos2, sin2, l, w_in_b, w_up_ret_b, w_pool_bd, pscale, w_up_pool_b, mk_bd, mv_bd,
                   w_up_x_b, w_out_b, decay, zeta, xi, gch)
        x = _ffn(x.reshape(batch * seq, d), l, w_ffn_in_b, w_ffn_out_b, fin,
                 final_norm=(l == depth - 1)).reshape(batch, seq, d)
    return x
```
